```python
import math
import jax, jax.numpy as jnp
from jax import lax
import numpy as np

D_MODEL = 2048
BATCH = 4
SEQ = 2048
DEPTH = 4
DEC_BATCH = 128
DEC_SEQ = 4
PAST_LEN = 8192
PAGE_SIZE = 128

MLA_NOPE = 128
MLA_ROPE = 64
MLA_V = 128
MLA_HEADS = (3 * D_MODEL) // (8 * MLA_V)
MLA_Q_RANK = D_MODEL // 4
MLA_KV_RANK = D_MODEL // 8
MLA_ROW = MLA_KV_RANK + MLA_ROPE
MLA_WIDTH = MLA_HEADS * MLA_V
MLA_SCALE = 1.0 / math.sqrt(MLA_NOPE + MLA_ROPE)
NSA_HD = 64
NSA_HEADS = (3 * D_MODEL) // (8 * NSA_HD)
NSA_KV_HEADS = 1
NSA_WIDTH = NSA_HEADS * NSA_HD
NSA_SCALE = 1.0 / math.sqrt(NSA_HD)
CMP_BLOCK = 32
CMP_HIDDEN = 128
SEL_BLOCK = 64
SEL_TOPK = 16
WINDOW = 512
SEL_FORCE = 1e4
CONV_CH = D_MODEL - MLA_WIDTH - NSA_WIDTH
CONV_W = 31
D_FF = 4 * D_MODEL
ROPE_THETA = 10000.0
EPS = 1e-6
Q_BLOCK = 128

OFF_CQ = 0
OFF_CKV = OFF_CQ + MLA_Q_RANK
OFF_KR = OFF_CKV + MLA_KV_RANK
OFF_NQ = OFF_KR + MLA_ROPE
OFF_NKV = OFF_NQ + NSA_WIDTH
OFF_GATE = OFF_NKV + 6 * NSA_KV_HEADS * NSA_HD
OFF_CONV = OFF_GATE + 3 * NSA_HEADS
IN_COLS = OFF_CONV + 2 * CONV_CH

kernel_name = "hymba_mla_nsa_conformer_decoder_step"


def rmsnorm(x, g):
    xf = x.astype(jnp.float32)
    y = xf * lax.rsqrt(jnp.mean(xf * xf, axis=-1, keepdims=True) + EPS)
    return (y * g.astype(jnp.float32)).astype(x.dtype)


def layernorm(x, g, b):
    xf = x.astype(jnp.float32)
    mu = jnp.mean(xf, axis=-1, keepdims=True)
    var = jnp.mean(jnp.square(xf - mu), axis=-1, keepdims=True)
    y = (xf - mu) * lax.rsqrt(var + EPS)
    return (y * g.astype(jnp.float32) + b.astype(jnp.float32)).astype(x.dtype)


def rope(x, pos):
    half = x.shape[-1] // 2
    freq = ROPE_THETA ** (-jnp.arange(half, dtype=jnp.float32) / half)
    ang = pos.astype(jnp.float32)[:, None] * freq[None, :]
    cos = jnp.cos(ang)[:, None, :]
    sin = jnp.sin(ang)[:, None, :]
    x1 = x[..., :half].astype(jnp.float32)
    x2 = x[..., half:].astype(jnp.float32)
    return jnp.concatenate([x1 * cos - x2 * sin, x1 * sin + x2 * cos], axis=-1).astype(x.dtype)


def masked_softmax(s, mask):
    s = jnp.where(mask, s.astype(jnp.float32), -jnp.inf)
    m = jnp.max(s, axis=-1, keepdims=True)
    m = jnp.where(jnp.isfinite(m), m, 0.0)
    p = jnp.exp(s - m)
    d = jnp.sum(p, axis=-1, keepdims=True)
    return p / jnp.where(d > 0, d, 1.0)


def to_blocks(x, nb):
    b, t = x.shape[:2]
    return jnp.moveaxis(x.reshape((b, nb, t // nb) + x.shape[2:]), 1, 0)


def from_blocks(y):
    y = jnp.moveaxis(y, 0, 1)
    return y.reshape((y.shape[0], y.shape[1] * y.shape[2]) + y.shape[3:])


def project(h, pos, w_in_l, q_norm_l, kv_norm_l, w_q_up_l, w_uk_l):
    b, t = h.shape[:2]
    z = h @ w_in_l
    cq = rmsnorm(z[..., OFF_CQ:OFF_CKV], q_norm_l)
    q = (cq @ w_q_up_l).reshape(b, t, MLA_HEADS, MLA_NOPE + MLA_ROPE)
    q_rope = rope(q[..., MLA_NOPE:], pos)
    q_lat = jnp.einsum('bqhn,rhn->bqhr', q[..., :MLA_NOPE], w_uk_l)
    ckv = rmsnorm(z[..., OFF_CKV:OFF_KR], kv_norm_l)
    k_rope = rope(z[..., OFF_KR:OFF_NQ][:, :, None, :], pos)[:, :, 0, :]
    mla_rows = jnp.concatenate([ckv, k_rope], axis=-1)
    nsa_q = rope(z[..., OFF_NQ:OFF_NKV].reshape(b, t, NSA_HEADS, NSA_HD), pos)
    kv = z[..., OFF_NKV:OFF_GATE].reshape(b, t, 3, 2, NSA_KV_HEADS, NSA_HD)
    k3 = rope(kv[:, :, :, 0].reshape(b, t, 3 * NSA_KV_HEADS, NSA_HD), pos).reshape(b, t, 3, NSA_KV_HEADS, NSA_HD)
    nsa_rows = jnp.stack([k3, kv[:, :, :, 1]], axis=3).reshape(b, t, 6, NSA_KV_HEADS, NSA_HD)
    gates = z[..., OFF_GATE:OFF_CONV].reshape(b, t, NSA_HEADS, 3)
    u = z[..., OFF_CONV:OFF_CONV + CONV_CH] * jax.nn.sigmoid(z[..., OFF_CONV + CONV_CH:])
    return q_lat, q_rope, mla_rows, nsa_q, nsa_rows, gates, u


def mla_attend(q_lat, q_rope, kv, k_pos, q_pos):
    ckv, kr = kv[..., :MLA_KV_RANK], kv[..., MLA_KV_RANK:]
    s = (jnp.einsum('bqhr,bkr->bhqk', q_lat, ckv) + jnp.einsum('bqhp,bkp->bhqk', q_rope, kr)).astype(jnp.float32) * MLA_SCALE
    p = masked_softmax(s, (k_pos[None, :] <= q_pos[:, None])[None, None])
    return jnp.einsum('bhqk,bkr->bqhr', p.astype(ckv.dtype), ckv)


def mla_prompt(q_lat, q_rope, rows):
    t = q_lat.shape[1]
    nb = t // Q_BLOCK
    k_pos = jnp.arange(t, dtype=jnp.int32)

    def one(args):
        ql, qr, b = args
        q_pos = b * Q_BLOCK + jnp.arange(Q_BLOCK, dtype=jnp.int32)
        return mla_attend(ql, qr, rows, k_pos, q_pos)

    out = lax.map(one, (to_blocks(q_lat, nb), to_blocks(q_rope, nb), jnp.arange(nb, dtype=jnp.int32)))
    return from_blocks(out)


def nsa_compress(rows, pe, w1, w2):
    b, l = rows.shape[:2]
    n = l // CMP_BLOCK
    blk = rows[:, :n * CMP_BLOCK].reshape(b, n, CMP_BLOCK, NSA_KV_HEADS, NSA_HD) + pe[None, None, :, None, :]
    flat = jnp.moveaxis(blk, 3, 2).reshape(b, n, NSA_KV_HEADS, CMP_BLOCK * NSA_HD)
    return jax.nn.gelu(flat @ w1) @ w2


def sel_blocks(rows):
    b, l = rows.shape[:2]
    n_sel = -(-l // SEL_BLOCK)
    rows = jnp.pad(rows, ((0, 0), (0, n_sel * SEL_BLOCK - l), (0, 0), (0, 0)))
    return rows.reshape(b, n_sel, SEL_BLOCK, NSA_KV_HEADS, NSA_HD)


def nsa_block(q, q_pos, gates, ck, cv, c_end, sk, sv, wk, wv, w_pos):
    b, tq = q.shape[:2]
    g = NSA_HEADS // NSA_KV_HEADS
    qg = q.reshape(b, tq, NSA_KV_HEADS, g, NSA_HD)
    s_c = jnp.einsum('bqhgd,bnhd->bhgqn', qg, ck).astype(jnp.float32) * NSA_SCALE
    p_c = masked_softmax(s_c, c_end[None, :] <= q_pos[:, None])
    o_c = jnp.einsum('bhgqn,bnhd->bqhgd', p_c.astype(cv.dtype), cv)
    n_sel = sk.shape[1]
    ratio = SEL_BLOCK // CMP_BLOCK
    imp = jnp.sum(p_c, axis=2)
    imp = jnp.pad(imp, ((0, 0), (0, 0), (0, 0), (0, n_sel * ratio - imp.shape[-1])))
    imp = imp.reshape(b, NSA_KV_HEADS, tq, n_sel, ratio).sum(-1)
    blk = jnp.arange(n_sel, dtype=jnp.int32)[None, :]
    cur = (q_pos // SEL_BLOCK)[:, None]
    valid = blk * SEL_BLOCK <= q_pos[:, None]
    forced = (blk == 0) | (blk == cur) | (blk == cur - 1)
    score = jnp.where(valid, jnp.where(forced, SEL_FORCE, imp), -jnp.inf)
    _, idx = lax.top_k(score, min(SEL_TOPK, n_sel))
    k = idx.shape[-1]
    gather = jax.vmap(jax.vmap(lambda kb, ix: kb[ix]))
    gk = gather(jnp.moveaxis(sk, 3, 1), idx)
    gv = gather(jnp.moveaxis(sv, 3, 1), idx)
    k_pos = idx[..., None] * SEL_BLOCK + jnp.arange(SEL_BLOCK, dtype=jnp.int32)
    m_s = (k_pos <= q_pos[None, None, :, None, None])[:, :, None].reshape(b, NSA_KV_HEADS, 1, tq, k * SEL_BLOCK)
    s_s = jnp.einsum('bqhgd,bhqksd->bhgqks', qg, gk).astype(jnp.float32) * NSA_SCALE
    p_s = masked_softmax(s_s.reshape(b, NSA_KV_HEADS, g, tq, k * SEL_BLOCK), m_s).reshape(b, NSA_KV_HEADS, g, tq, k, SEL_BLOCK)
    o_s = jnp.einsum('bhgqks,bhqksd->bqhgd', p_s.astype(gv.dtype), gv)
    dist = q_pos[:, None] - w_pos[None, :]
    m_w = (dist >= 0) & (dist < WINDOW) & (w_pos[None, :] >= 0)
    s_w = jnp.einsum('bqhgd,bshd->bhgqs', qg, wk).astype(jnp.float32) * NSA_SCALE
    p_w = masked_softmax(s_w, m_w)
    o_w = jnp.einsum('bhgqs,bshd->bqhgd', p_w.astype(wv.dtype), wv)
    gt = jax.nn.sigmoid(gates.astype(jnp.float32)).astype(q.dtype).reshape(b, tq, NSA_KV_HEADS, g, 3)
    o = gt[..., 0:1] * o_c + gt[..., 1:2] * o_s + gt[..., 2:3] * o_w
    return o.reshape(b, tq, NSA_HEADS, NSA_HD)


def nsa_prompt(q, gates, rows, pe, w1, w2, win_buf):
    t = q.shape[1]
    ck = nsa_compress(rows[:, :, 0], pe[0], w1[0], w2[0])
    cv = nsa_compress(rows[:, :, 1], pe[1], w1[1], w2[1])
    c_end = jnp.arange(ck.shape[1], dtype=jnp.int32) * CMP_BLOCK + (CMP_BLOCK - 1)
    sk, sv = sel_blocks(rows[:, :, 2]), sel_blocks(rows[:, :, 3])
    wpad = jnp.pad(rows[:, :, 4:6], ((0, 0), (WINDOW, 0), (0, 0), (0, 0), (0, 0)))
    nb = t // Q_BLOCK

    def one(args):
        qb, gb, bi = args
        start = bi * Q_BLOCK
        q_pos = start + jnp.arange(Q_BLOCK, dtype=jnp.int32)
        wblk = lax.dynamic_slice_in_dim(wpad, start, WINDOW + Q_BLOCK, axis=1)
        w_pos = start - WINDOW + jnp.arange(WINDOW + Q_BLOCK, dtype=jnp.int32)
        return nsa_block(qb, q_pos, gb, ck, cv, c_end, sk, sv, wblk[:, :, 0], wblk[:, :, 1], w_pos)

    out = lax.map(one, (to_blocks(q, nb), to_blocks(gates, nb), jnp.arange(nb, dtype=jnp.int32)))
    return from_blocks(out), wpad[:, wpad.shape[1] - win_buf:]


def nsa_sample(q, gates, rows, past, win_state, q_pos, pe, w1, w2, win_buf):
    b, t = q.shape[:2]
    full = jnp.concatenate([past.reshape(b, PAST_LEN, 4, NSA_KV_HEADS, NSA_HD), rows[:, :, :4]], axis=1)
    ck = nsa_compress(full[:, :, 0], pe[0], w1[0], w2[0])
    cv = nsa_compress(full[:, :, 1], pe[1], w1[1], w2[1])
    c_end = jnp.arange(ck.shape[1], dtype=jnp.int32) * CMP_BLOCK + (CMP_BLOCK - 1)
    sk, sv = sel_blocks(full[:, :, 2]), sel_blocks(full[:, :, 3])
    wbuf = jnp.concatenate([win_state, rows[:, :, 4:6]], axis=1)
    w_pos = (PAST_LEN - win_buf) + jnp.arange(win_buf + t, dtype=jnp.int32)
    o = nsa_block(q, q_pos, gates, ck, cv, c_end, sk, sv, wbuf[:, :, 0], wbuf[:, :, 1], w_pos)
    return o, wbuf[:, t:]


def conv_module(buf, w, bias, g, beta):
    y = lax.conv_general_dilated(buf, w[:, None, :], window_strides=(1,), padding='VALID',
                                 dimension_numbers=('NWC', 'WIO', 'NWC'), feature_group_count=CONV_CH) + bias
    return jax.nn.silu(layernorm(y, g, beta))


def merge(o_lat, w_uv_l, o_nsa, o_conv, gn, w_out_l):
    b, t = o_lat.shape[:2]
    o_mla = jnp.einsum('bqhr,rhv->bqhv', o_lat, w_uv_l).reshape(b, t, MLA_WIDTH)
    parts = jnp.concatenate([
        rmsnorm(o_mla, gn[:MLA_WIDTH]),
        rmsnorm(o_nsa.reshape(b, t, NSA_WIDTH), gn[MLA_WIDTH:MLA_WIDTH + NSA_WIDTH]),
        rmsnorm(o_conv, gn[MLA_WIDTH + NSA_WIDTH:]),
    ], axis=-1)
    return parts @ w_out_l


def sqrelu_mlp(h, w1, w2):
    return jnp.square(jax.nn.relu(h @ w1)) @ w2


def setup_inputs(seed: int = 0) -> dict:
    key = jax.random.key(seed)
    ks = jax.random.split(key, 32)
    n_pages = PAST_LEN // PAGE_SIZE
    n_used = DEC_BATCH * n_pages
    n_pool = n_used + n_used // 4
    win_buf = min(WINDOW, PAST_LEN)
    f32 = jnp.float32

    def nrm(k, shape, scale=1.0):
        return jax.random.normal(k, shape, f32) * scale

    def gain(k, shape):
        return 1.0 + 0.05 * jax.random.normal(k, shape, f32)

    page_table = jax.random.permutation(ks[0], n_pool)[:n_used].reshape(DEC_BATCH, n_pages).astype(jnp.int32)
    return {
        "x_prompt": nrm(ks[1], (BATCH, SEQ, D_MODEL)),
        "x_sample": nrm(ks[2], (DEC_BATCH, DEC_SEQ, D_MODEL)),
        "cache_mla": nrm(ks[3], (DEPTH, n_pool, PAGE_SIZE, MLA_ROW)),
        "cache_nsa": nrm(ks[4], (DEPTH, n_pool, PAGE_SIZE, 4, NSA_KV_HEADS, NSA_HD)),
        "state_nsa_win": nrm(ks[5], (DEPTH, DEC_BATCH, win_buf, 2, NSA_KV_HEADS, NSA_HD)),
        "state_conv": nrm(ks[6], (DEPTH, DEC_BATCH, CONV_W - 1, CONV_CH)),
        "page_table": page_table,
        "norm_mix": gain(ks[7], (DEPTH, D_MODEL)),
        "w_in": nrm(ks[8], (DEPTH, D_MODEL, IN_COLS), D_MODEL ** -0.5),
        "mla_q_norm": gain(ks[9], (DEPTH, MLA_Q_RANK)),
        "mla_kv_norm": gain(ks[10], (DEPTH, MLA_KV_RANK)),
        "mla_w_q_up": nrm(ks[11], (DEPTH, MLA_Q_RANK, MLA_HEADS * (MLA_NOPE + MLA_ROPE)), MLA_Q_RANK ** -0.5),
        "mla_w_uk": nrm(ks[12], (DEPTH, MLA_KV_RANK, MLA_HEADS, MLA_NOPE), MLA_KV_RANK ** -0.5),
        "mla_w_uv": nrm(ks[13], (DEPTH, MLA_KV_RANK, MLA_HEADS, MLA_V), MLA_KV_RANK ** -0.5),
        "nsa_cmp_pe": nrm(ks[14], (DEPTH, 2, CMP_BLOCK, NSA_HD), 0.1),
        "nsa_cmp_w1": nrm(ks[15], (DEPTH, 2, CMP_BLOCK * NSA_HD, CMP_HIDDEN), (CMP_BLOCK * NSA_HD) ** -0.5),
        "nsa_cmp_w2": nrm(ks[16], (DEPTH, 2, CMP_HIDDEN, NSA_HD), CMP_HIDDEN ** -0.5),
        "conv_w": nrm(ks[17], (DEPTH, CONV_W, CONV_CH), CONV_W ** -0.5),
        "conv_b": nrm(ks[18], (DEPTH, CONV_CH), 0.02),
        "conv_ln_g": gain(ks[19], (DEPTH, CONV_CH)),
        "conv_ln_b": nrm(ks[20], (DEPTH, CONV_CH), 0.02),
        "group_norm": gain(ks[21], (DEPTH, D_MODEL)),
        "w_out": nrm(ks[22], (DEPTH, D_MODEL, D_MODEL), D_MODEL ** -0.5),
        "norm_mlp": gain(ks[23], (DEPTH, D_MODEL)),
        "mlp_w1": nrm(ks[24], (DEPTH, D_MODEL, D_FF), D_MODEL ** -0.5),
        "mlp_w2": nrm(ks[25], (DEPTH, D_FF, D_MODEL), D_FF ** -0.5),
        "norm_final": gain(ks[26], (D_MODEL,)),
    }


def reference(x_prompt, x_sample, cache_mla, cache_nsa, state_nsa_win, state_conv, page_table,
              norm_mix, w_in, mla_q_norm, mla_kv_norm, mla_w_q_up, mla_w_uk, mla_w_uv,
              nsa_cmp_pe, nsa_cmp_w1, nsa_cmp_w2, conv_w, conv_b, conv_ln_g, conv_ln_b,
              group_norm, w_out, norm_mlp, mlp_w1, mlp_w2, norm_final):
    win_buf = min(WINDOW, PAST_LEN)
    pos_p = jnp.arange(SEQ, dtype=jnp.int32)
    pos_s = PAST_LEN + jnp.arange(DEC_SEQ, dtype=jnp.int32)
    k_pos_s = jnp.arange(PAST_LEN + DEC_SEQ, dtype=jnp.int32)
    xp, xs = x_prompt, x_sample
    mla_p, mla_s, nsa_p, nsa_s, win_p, win_s, conv_p, conv_s = [], [], [], [], [], [], [], []
    for l in range(DEPTH):
        hp = rmsnorm(xp, norm_mix[l])
        hs = rmsnorm(xs, norm_mix[l])
        qlp, qrp, mrp, nqp, nkp, gp, up = project(hp, pos_p, w_in[l], mla_q_norm[l], mla_kv_norm[l], mla_w_q_up[l], mla_w_uk[l])
        qls, qrs, mrs, nqs, nks, gs, us = project(hs, pos_s, w_in[l], mla_q_norm[l], mla_kv_norm[l], mla_w_q_up[l], mla_w_uk[l])
        olp = mla_prompt(qlp, qrp, mrp)
        past_mla = cache_mla[l][page_table].reshape(DEC_BATCH, PAST_LEN, MLA_ROW)
        ols = mla_attend(qls, qrs, jnp.concatenate([past_mla, mrs], axis=1), k_pos_s, pos_s)
        onp, wnp = nsa_prompt(nqp, gp, nkp, nsa_cmp_pe[l], nsa_cmp_w1[l], nsa_cmp_w2[l], win_buf)
        ons, wns = nsa_sample(nqs, gs, nks, cache_nsa[l][page_table], state_nsa_win[l], pos_s,
                              nsa_cmp_pe[l], nsa_cmp_w1[l], nsa_cmp_w2[l], win_buf)
        bufp = jnp.pad(up, ((0, 0), (CONV_W - 1, 0), (0, 0)))
        bufs = jnp.concatenate([state_conv[l], us], axis=1)
        ocp = conv_module(bufp, conv_w[l], conv_b[l], conv_ln_g[l], conv_ln_b[l])
        ocs = conv_module(bufs, conv_w[l], conv_b[l], conv_ln_g[l], conv_ln_b[l])
        xp = xp + merge(olp, mla_w_uv[l], onp, ocp, group_norm[l], w_out[l])
        xs = xs + merge(ols, mla_w_uv[l], ons, ocs, group_norm[l], w_out[l])
        xp = xp + sqrelu_mlp(rmsnorm(xp, norm_mlp[l]), mlp_w1[l], mlp_w2[l])
        xs = xs + sqrelu_mlp(rmsnorm(xs, norm_mlp[l]), mlp_w1[l], mlp_w2[l])
        mla_p.append(mrp)
        mla_s.append(mrs)
        nsa_p.append(nkp[:, :, :4])
        nsa_s.append(nks[:, :, :4])
        win_p.append(wnp)
        win_s.append(wns)
        conv_p.append(bufp[:, bufp.shape[1] - (CONV_W - 1):])
        conv_s.append(bufs[:, bufs.shape[1] - (CONV_W - 1):])
    y_prompt = rmsnorm(xp, norm_final)
    y_sample = rmsnorm(xs, norm_final)
    return (y_prompt, y_sample,
            jnp.stack(mla_p), jnp.stack(mla_s),
            jnp.stack(nsa_p), jnp.stack(nsa_s),
            jnp.stack(win_p), jnp.stack(win_s),
            jnp.stack(conv_p), jnp.stack(conv_s))
```

```python
import functools
import math

import jax
import jax.numpy as jnp
from jax import lax
from jax.experimental import pallas as pl
from jax.experimental.pallas import tpu as pltpu

F32 = jnp.float32
BF16 = jnp.bfloat16

D_MODEL = 2048
BATCH = 4
SEQ = 2048
DEPTH = 4
DEC_BATCH = 128
DEC_SEQ = 4
PAST_LEN = 8192
PAGE_SIZE = 128
N_PAGES = PAST_LEN // PAGE_SIZE
MLA_NOPE = 128
MLA_ROPE = 64
MLA_V = 128
MLA_HEADS = 6
MLA_Q_RANK = 512
MLA_KV_RANK = 256
MLA_ROW = MLA_KV_RANK + MLA_ROPE
MLA_WIDTH = MLA_HEADS * MLA_V
MLA_SCALE = 1.0 / math.sqrt(MLA_NOPE + MLA_ROPE)
NSA_HD = 64
NSA_HEADS = 12
NSA_WIDTH = NSA_HEADS * NSA_HD
NSA_SCALE = 1.0 / math.sqrt(NSA_HD)
CMP_BLOCK = 32
CMP_HIDDEN = 128
SEL_BLOCK = 64
SEL_TOPK = 16
WINDOW = 512
SEL_FORCE = 1e4
CONV_CH = 512
CONV_W = 31
D_FF = 4 * D_MODEL
ROPE_THETA = 10000.0
EPS = 1e-6

OFF_CQ = 0
OFF_CKV = OFF_CQ + MLA_Q_RANK
OFF_KR = OFF_CKV + MLA_KV_RANK
OFF_NQ = OFF_KR + MLA_ROPE
OFF_NKV = OFF_NQ + NSA_WIDTH
OFF_GATE = OFF_NKV + 6 * NSA_HD
OFF_CONV = OFF_GATE + 3 * NSA_HEADS
IN_COLS = OFF_CONV + 2 * CONV_CH

LANE = 128
VMEM_LIMIT = 56 * 1024 * 1024

MP = BATCH * SEQ
MS = DEC_BATCH * DEC_SEQ
MT = MP + MS
TS = 8
KV_PAD = 3 * LANE

Z_CQ = 0
Z_CKV = 512
Z_KR = 768
Z_NQ = 896
Z_NKV = 1664
Z_GATE = 2048
Z_CA = 2176
Z_CB = 2688
NZ = 3200

NEG = -1e30


def _dot(a, b):
    return jnp.dot(a, b, preferred_element_type=F32)


def _dot_nt(a, b):
    return lax.dot_general(a, b, (((1,), (1,)), ((), ())), preferred_element_type=F32)


def _rms(x, g):
    return x * lax.rsqrt(jnp.mean(x * x, axis=-1, keepdims=True) + EPS) * g


def _params(sem):
    return pltpu.CompilerParams(dimension_semantics=sem, vmem_limit_bytes=VMEM_LIMIT)


def _inproj_body(x_ref, g_ref, w_ref, o_ref, xn_ref):
    @pl.when(pl.program_id(1) == 0)
    def _():
        xn_ref[...] = _rms(x_ref[...], g_ref[...]).astype(BF16)

    o_ref[...] = _dot(xn_ref[...], w_ref[...])


def _inproj(x, g, w, *, tm=512, tn=640):
    m = x.shape[0]
    return pl.pallas_call(
        _inproj_body,
        grid=(m // tm, NZ // tn),
        in_specs=[
            pl.BlockSpec((tm, D_MODEL), lambda i, j: (i, 0)),
            pl.BlockSpec((1, D_MODEL), lambda i, j: (0, 0)),
            pl.BlockSpec((D_MODEL, tn), lambda i, j: (0, j)),
        ],
        out_specs=pl.BlockSpec((tm, tn), lambda i, j: (i, j)),
        out_shape=jax.ShapeDtypeStruct((m, NZ), F32),
        scratch_shapes=[pltpu.VMEM((tm, D_MODEL), BF16)],
        compiler_params=_params(("arbitrary", "arbitrary")),
        name="inproj",
    )(x, g, w)


def _post_body(z_ref, cos_ref, sin_ref, qn_ref, kvn_ref, wq_ref, wuk_ref,
               q_ref, kvb_ref, mla_ref, nq_ref, rows_ref, rowsb_ref, gate_ref, u_ref, *, tm):
    cos = cos_ref[...]
    sin = sin_ref[...]
    lane = lax.broadcasted_iota(jnp.int32, (tm, LANE), 1)
    first = (lane & 63) < 32
    lo = lane < 64
    sa = jnp.where(first, -sin, 0.0)
    sb = jnp.where(first, 0.0, sin)
    ch = jnp.where(lo, cos, 1.0)
    sah = jnp.where(lo, sa, 0.0)
    sbh = jnp.where(lo, sb, 0.0)

    def rope_full(x):
        return x * cos + pltpu.roll(x, 96, 1) * sa + pltpu.roll(x, 32, 1) * sb

    def rope_half(x):
        return x * ch + pltpu.roll(x, 96, 1) * sah + pltpu.roll(x, 32, 1) * sbh

    def split_heads(t):
        return jnp.where(lo, t, 0.0), jnp.where(lo, pltpu.roll(t, 64, 1), 0.0)

    cq = _rms(z_ref[:, Z_CQ:Z_CQ + MLA_Q_RANK], qn_ref[...]).astype(BF16)
    q = _dot(cq, wq_ref[...])
    for h in range(MLA_HEADS):
        ql = _dot(q[:, h * MLA_NOPE:(h + 1) * MLA_NOPE].astype(BF16), wuk_ref[h])
        q_ref[h, :, 0:MLA_KV_RANK] = (ql * MLA_SCALE).astype(BF16)
    rope0 = MLA_HEADS * MLA_NOPE
    for j in range(MLA_HEADS // 2):
        t = rope_full(q[:, rope0 + j * LANE:rope0 + (j + 1) * LANE]) * MLA_SCALE
        a, b = split_heads(t)
        q_ref[2 * j, :, MLA_KV_RANK:KV_PAD] = a.astype(BF16)
        q_ref[2 * j + 1, :, MLA_KV_RANK:KV_PAD] = b.astype(BF16)

    ckv = _rms(z_ref[:, Z_CKV:Z_CKV + MLA_KV_RANK], kvn_ref[...])
    kr = rope_half(z_ref[:, Z_KR:Z_KR + LANE])
    mla_ref[:, 0:MLA_KV_RANK] = ckv
    mla_ref[:, MLA_KV_RANK:KV_PAD] = kr
    kvb_ref[:, 0:MLA_KV_RANK] = ckv.astype(BF16)
    kvb_ref[:, MLA_KV_RANK:KV_PAD] = kr.astype(BF16)

    for j in range(NSA_HEADS // 2):
        t = rope_full(z_ref[:, Z_NQ + j * LANE:Z_NQ + (j + 1) * LANE]) * NSA_SCALE
        a, b = split_heads(t)
        nq_ref[2 * j] = a.astype(BF16)
        nq_ref[2 * j + 1] = b.astype(BF16)

    for j in range(3):
        t = rope_half(z_ref[:, Z_NKV + j * LANE:Z_NKV + (j + 1) * LANE])
        rows_ref[:, j * LANE:(j + 1) * LANE] = t
        rowsb_ref[:, j * LANE:(j + 1) * LANE] = t.astype(BF16)

    gate_ref[...] = jax.nn.sigmoid(z_ref[:, Z_GATE:Z_GATE + LANE])
    u_ref[...] = z_ref[:, Z_CA:Z_CA + CONV_CH] * jax.nn.sigmoid(z_ref[:, Z_CB:Z_CB + CONV_CH])


def _post(z, cos_t, sin_t, qn, kvn, wq, wuk, *, tm=256):
    m = z.shape[0]
    n_prompt_tiles = MP // tm
    tiles_per_seq = SEQ // tm

    def tab_map(i):
        return (jnp.where(i < n_prompt_tiles, i % tiles_per_seq, tiles_per_seq), 0)

    row = lambda i: (i, 0)
    const2 = lambda i: (0, 0)
    return pl.pallas_call(
        functools.partial(_post_body, tm=tm),
        grid=(m // tm,),
        in_specs=[
            pl.BlockSpec((tm, NZ), row),
            pl.BlockSpec((tm, LANE), tab_map),
            pl.BlockSpec((tm, LANE), tab_map),
            pl.BlockSpec((1, MLA_Q_RANK), const2),
            pl.BlockSpec((1, MLA_KV_RANK), const2),
            pl.BlockSpec((MLA_Q_RANK, MLA_HEADS * (MLA_NOPE + MLA_ROPE)), const2),
            pl.BlockSpec((MLA_HEADS, MLA_NOPE, MLA_KV_RANK), lambda i: (0, 0, 0)),
        ],
        out_specs=[
            pl.BlockSpec((MLA_HEADS, tm, KV_PAD), lambda i: (0, i, 0)),
            pl.BlockSpec((tm, KV_PAD), row),
            pl.BlockSpec((tm, KV_PAD), row),
            pl.BlockSpec((NSA_HEADS, tm, LANE), lambda i: (0, i, 0)),
            pl.BlockSpec((tm, 3 * LANE), row),
            pl.BlockSpec((tm, 3 * LANE), row),
            pl.BlockSpec((tm, LANE), row),
            pl.BlockSpec((tm, CONV_CH), row),
        ],
        out_shape=[
            jax.ShapeDtypeStruct((MLA_HEADS, m, KV_PAD), BF16),
            jax.ShapeDtypeStruct((m, KV_PAD), BF16),
            jax.ShapeDtypeStruct((m, KV_PAD), F32),
            jax.ShapeDtypeStruct((NSA_HEADS, m, LANE), BF16),
            jax.ShapeDtypeStruct((m, 3 * LANE), F32),
            jax.ShapeDtypeStruct((m, 3 * LANE), BF16),
            jax.ShapeDtypeStruct((m, LANE), F32),
            jax.ShapeDtypeStruct((m, CONV_CH), F32),
        ],
        compiler_params=_params(("arbitrary",)),
        name="post_proj",
    )(z, cos_t, sin_t, qn, kvn, wq, wuk)


def _mla_prompt_body(q_ref, kv_ref, o_ref, m_ref, l_ref, acc_ref, *, tq, tk):
    qi = pl.program_id(1)
    r = MLA_HEADS * tq
    q = q_ref[...].reshape(r, KV_PAD)
    m_ref[...] = jnp.full((r, 1), NEG, F32)
    l_ref[...] = jnp.zeros((r, 1), F32)
    acc_ref[...] = jnp.zeros((r, MLA_KV_RANK), F32)
    qpos = qi * tq + (lax.broadcasted_iota(jnp.int32, (r, tk), 0) & (tq - 1))
    col = lax.broadcasted_iota(jnp.int32, (r, tk), 1)
    nchunks = (qi * tq + tq + tk - 1) // tk

    def step(j, carry):
        k = kv_ref[pl.ds(pl.multiple_of(j * tk, tk), tk), :]
        s = _dot_nt(q, k)
        s = jnp.where(j * tk + col <= qpos, s, NEG)
        m_old = m_ref[...]
        m_new = jnp.maximum(m_old, jnp.max(s, axis=1, keepdims=True))
        alpha = jnp.exp(m_old - m_new)
        p = jnp.exp(s - m_new)
        l_ref[...] = alpha * l_ref[...] + jnp.sum(p, axis=1, keepdims=True)
        acc_ref[...] = alpha * acc_ref[...] + _dot(p.astype(BF16), k[:, 0:MLA_KV_RANK])
        m_ref[...] = m_new
        return carry

    lax.fori_loop(0, nchunks, step, 0)
    o = acc_ref[...] / l_ref[...]
    for h in range(MLA_HEADS):
        o_ref[:, h * MLA_KV_RANK:(h + 1) * MLA_KV_RANK] = o[h * tq:(h + 1) * tq].astype(BF16)


def _mla_prompt(q, kvb, *, tq=128, tk=512):
    nq = SEQ // tq
    r = MLA_HEADS * tq
    return pl.pallas_call(
        functools.partial(_mla_prompt_body, tq=tq, tk=tk),
        grid=(BATCH, nq),
        in_specs=[
            pl.BlockSpec((MLA_HEADS, tq, KV_PAD), lambda b, i: (0, b * nq + i, 0)),
            pl.BlockSpec((SEQ, KV_PAD), lambda b, i: (b, 0)),
        ],
        out_specs=pl.BlockSpec((tq, MLA_HEADS * MLA_KV_RANK), lambda b, i: (b * nq + i, 0)),
        out_shape=jax.ShapeDtypeStruct((MP, MLA_HEADS * MLA_KV_RANK), BF16),
        scratch_shapes=[pltpu.VMEM((r, 1), F32), pltpu.VMEM((r, 1), F32), pltpu.VMEM((r, MLA_KV_RANK), F32)],
        compiler_params=_params(("arbitrary", "arbitrary")),
        name="mla_prompt",
    )(q, kvb)


def _mla_sample_body(pt_ref, q_ref, knew_ref, *rest):
    pages = rest[:N_PAGES]
    o_ref = rest[N_PAGES]
    kt_ref = rest[N_PAGES + 1]
    lk = PAST_LEN + PAGE_SIZE
    kt_ref[MLA_ROW:KV_PAD, :] = jnp.zeros((KV_PAD - MLA_ROW, lk), BF16)
    for i in range(N_PAGES):
        kt_ref[0:MLA_ROW, i * PAGE_SIZE:(i + 1) * PAGE_SIZE] = pages[i][...].astype(BF16)
    kt_ref[:, PAST_LEN:lk] = knew_ref[0]
    q = q_ref[0]
    r = q.shape[0]
    s = _dot(q, kt_ref[...])
    col = lax.broadcasted_iota(jnp.int32, (r, lk), 1)
    row = lax.broadcasted_iota(jnp.int32, (r, lk), 0)
    valid = (col < PAST_LEN) | ((col - PAST_LEN) <= (row & (DEC_SEQ - 1)))
    s = jnp.where(valid, s, NEG)
    m = jnp.max(s, axis=1, keepdims=True)
    p = jnp.exp(s - m)
    l = jnp.sum(p, axis=1, keepdims=True)
    o = _dot_nt(p.astype(BF16), kt_ref[0:MLA_KV_RANK, :]) / l
    o_ref[0] = o.astype(BF16)


def _mla_sample(page_flat, qs, knew_t, cache_mla_t, layer):
    r = MLA_HEADS * DEC_SEQ

    def page_spec(i):
        return pl.BlockSpec((None, None, MLA_ROW, PAGE_SIZE),
                            lambda b, pt: (layer, pt[b * N_PAGES + i], 0, 0))

    grid_spec = pltpu.PrefetchScalarGridSpec(
        num_scalar_prefetch=1,
        grid=(DEC_BATCH,),
        in_specs=[
            pl.BlockSpec((1, r, KV_PAD), lambda b, pt: (b, 0, 0)),
            pl.BlockSpec((1, KV_PAD, PAGE_SIZE), lambda b, pt: (b, 0, 0)),
        ] + [page_spec(i) for i in range(N_PAGES)],
        out_specs=pl.BlockSpec((1, r, MLA_KV_RANK), lambda b, pt: (b, 0, 0)),
        scratch_shapes=[pltpu.VMEM((KV_PAD, PAST_LEN + PAGE_SIZE), BF16)],
    )
    return pl.pallas_call(
        _mla_sample_body,
        grid_spec=grid_spec,
        out_shape=jax.ShapeDtypeStruct((DEC_BATCH, r, MLA_KV_RANK), BF16),
        compiler_params=_params(("arbitrary",)),
        name="mla_sample",
    )(page_flat, qs, knew_t, *([cache_mla_t] * N_PAGES))


def _gelu_tanh(x):
    return 0.5 * x * (1.0 + jnp.tanh(math.sqrt(2.0 / math.pi) * (x + 0.044715 * (x * x * x))))


_PAIR = 2 * PAGE_SIZE
_BPP = _PAIR // CMP_BLOCK
_NCL = 2 * LANE


def _scatter_pair(gall, g_ref, pp):
    for r in range(CMP_BLOCK):
        g_ref[r, pp * _BPP:(pp + 1) * _BPP, :] = gall[r * _BPP:(r + 1) * _BPP, :]


def _compress_finish(g_ref, pe_ref, w1_ref, w2_ref, peo_ref, o_ref, *, nb):
    for r in range(CMP_BLOCK):
        g_ref[r, nb:nb + 8, :] = jnp.broadcast_to(pe_ref[r:r + 1, :], (8, LANE))
    h = jnp.zeros((nb + 8, 2 * CMP_HIDDEN), F32)
    for r in range(CMP_BLOCK):
        h = h + _dot(g_ref[r].astype(BF16), w1_ref[r])
    hb = h[0:nb] + h[nb:nb + 1]
    c = _dot(_gelu_tanh(hb).astype(BF16), w2_ref[...]).astype(BF16)
    o_ref[0] = _dot(peo_ref[...], c).astype(BF16)


def _compress_prompt_body(src_ref, perm_ref, pe_ref, w1_ref, w2_ref, peo_ref, o_ref, g_ref):
    for pp in range(SEQ // _PAIR):
        x = src_ref[pp * _PAIR:(pp + 1) * _PAIR, :].astype(BF16)
        _scatter_pair(_dot(perm_ref[...], x), g_ref, pp)
    _compress_finish(g_ref, pe_ref, w1_ref, w2_ref, peo_ref, o_ref, nb=SEQ // CMP_BLOCK)


def _compress_prompt(rows, perm, pe, w1, w2, peo):
    nb = SEQ // CMP_BLOCK
    c2 = lambda b: (0, 0)
    return pl.pallas_call(
        _compress_prompt_body,
        grid=(BATCH,),
        in_specs=[
            pl.BlockSpec((SEQ, LANE), lambda b: (b, 0)),
            pl.BlockSpec((_PAIR, _PAIR), c2),
            pl.BlockSpec((CMP_BLOCK, LANE), c2),
            pl.BlockSpec((CMP_BLOCK, LANE, 2 * CMP_HIDDEN), lambda b: (0, 0, 0)),
            pl.BlockSpec((2 * CMP_HIDDEN, LANE), c2),
            pl.BlockSpec((_NCL, nb), c2),
        ],
        out_specs=pl.BlockSpec((1, _NCL, LANE), lambda b: (b, 0, 0)),
        out_shape=jax.ShapeDtypeStruct((BATCH, _NCL, LANE), BF16),
        scratch_shapes=[pltpu.VMEM((CMP_BLOCK, nb + 8, LANE), F32)],
        compiler_params=_params(("arbitrary",)),
        name="compress_prompt",
    )(rows, perm, pe, w1, w2, peo)


def _compress_sample_body(pt_ref, perm_ref, pe_ref, w1_ref, w2_ref, peo_ref, *rest):
    pages = rest[:N_PAGES]
    o_ref = rest[N_PAGES]
    g_ref = rest[N_PAGES + 1]
    for pp in range(N_PAGES // 2):
        xt = jnp.concatenate([pages[2 * pp][...], pages[2 * pp + 1][...]], axis=1).astype(BF16)
        _scatter_pair(_dot_nt(perm_ref[...], xt), g_ref, pp)
    _compress_finish(g_ref, pe_ref, w1_ref, w2_ref, peo_ref, o_ref, nb=PAST_LEN // CMP_BLOCK)


def _compress_sample(page_flat, perm, pe, w1, w2, peo, cache_nsa_t, layer):
    nb = PAST_LEN // CMP_BLOCK

    def page_spec(i):
        return pl.BlockSpec((None, None, LANE, PAGE_SIZE),
                            lambda b, pt: (layer, pt[b * N_PAGES + i], 0, 0))

    c2 = lambda b, pt: (0, 0)
    grid_spec = pltpu.PrefetchScalarGridSpec(
        num_scalar_prefetch=1,
        grid=(DEC_BATCH,),
        in_specs=[
            pl.BlockSpec((_PAIR, _PAIR), c2),
            pl.BlockSpec((CMP_BLOCK, LANE), c2),
            pl.BlockSpec((CMP_BLOCK, LANE, 2 * CMP_HIDDEN), lambda b, pt: (0, 0, 0)),
            pl.BlockSpec((2 * CMP_HIDDEN, LANE), c2),
            pl.BlockSpec((_NCL, nb), c2),
        ] + [page_spec(i) for i in range(N_PAGES)],
        out_specs=pl.BlockSpec((1, _NCL, LANE), lambda b, pt: (b, 0, 0)),
        scratch_shapes=[pltpu.VMEM((CMP_BLOCK, nb + 8, LANE), F32)],
    )
    return pl.pallas_call(
        _compress_sample_body,
        grid_spec=grid_spec,
        out_shape=jax.ShapeDtypeStruct((DEC_BATCH, _NCL, LANE), BF16),
        compiler_params=_params(("arbitrary",)),
        name="compress_sample",
    )(page_flat, perm, pe, w1, w2, peo, *([cache_nsa_t] * N_PAGES))


def _perm_matrices(nb):
    i = jnp.arange(_PAIR)
    perm = (jnp.arange(_PAIR)[None, :] == (CMP_BLOCK * (i % _BPP) + i // _BPP)[:, None]).astype(BF16)
    j = jnp.arange(_NCL)
    blk = jnp.where(j < _NCL // 2, 2 * j, 2 * (j - _NCL // 2) + 1)
    peo = (jnp.arange(nb)[None, :] == blk[:, None]).astype(BF16)
    return perm, peo


def _nsa_core(q, tpos, ckv, get_chunk, nchunks, tk, wkv, wpos0, gate,
              m_ref, l_ref, acc_ref, *, tq, n_cmp, n_sel, nsl, keys_on_lanes):
    if keys_on_lanes:
        qk = _dot
        pv = _dot_nt
    else:
        qk = _dot_nt
        pv = _dot
    r = NSA_HEADS * tq
    qpos = jnp.concatenate([tpos] * NSA_HEADS, axis=0)

    ncl = ckv.shape[0]
    half = ncl // 2
    s_c = _dot_nt(q, ckv)
    jc = lax.broadcasted_iota(jnp.int32, (1, ncl), 1)
    blk = jnp.where(jc < half, 2 * jc, 2 * (jc - half) + 1)
    valid_c = (blk < n_cmp) & (blk * CMP_BLOCK + (CMP_BLOCK - 1) <= qpos)
    s_c = jnp.where(valid_c, s_c, NEG)
    m_c = jnp.max(s_c, axis=1, keepdims=True)
    p = jnp.where(valid_c, jnp.exp(s_c - m_c), 0.0)
    d = jnp.sum(p, axis=1, keepdims=True)
    p_c = p / jnp.where(d > 0, d, 1.0)
    o_c = _dot(p_c.astype(BF16), ckv)

    imp = p_c[0:tq]
    for h in range(1, NSA_HEADS):
        imp = imp + p_c[h * tq:(h + 1) * tq]
    imp_sel = imp[:, 0:half] + imp[:, half:ncl]
    if nsl > half:
        imp_sel = jnp.concatenate([imp_sel, jnp.zeros((tq, nsl - half), F32)], axis=1)
    mm = lax.broadcasted_iota(jnp.int32, (tq, nsl), 1)
    cur = tpos >> 6
    valid_s = (mm * SEL_BLOCK <= tpos) & (mm < n_sel)
    forced = (mm == 0) | (mm == cur) | (mm == cur - 1)
    score = jnp.where(valid_s, jnp.where(forced, SEL_FORCE, imp_sel), -jnp.inf)
    rank = jnp.zeros((tq, nsl), F32)
    for j in range(n_sel):
        sc = score[:, j:j + 1]
        ahead = (sc > score) | ((sc == score) & (mm > j))
        rank = rank + jnp.where(ahead, 1.0, 0.0)
    selb = jnp.where(rank < float(min(SEL_TOPK, n_sel)), 1.0, 0.0).astype(BF16)

    m_ref[...] = jnp.full((r, 1), NEG, F32)
    l_ref[...] = jnp.zeros((r, 1), F32)
    acc_ref[...] = jnp.zeros((r, LANE), F32)
    bpc = tk // SEL_BLOCK
    em = lax.broadcasted_iota(jnp.int32, (nsl, tk), 0)
    ek = lax.broadcasted_iota(jnp.int32, (nsl, tk), 1) >> 6
    kcol = lax.broadcasted_iota(jnp.int32, (tq, tk), 1)

    def step(c, carry):
        k = get_chunk(c)
        expand = jnp.where(em == ek + c * bpc, 1.0, 0.0).astype(BF16)
        chosen = _dot(selb, expand) > 0.5
        mask = chosen & (kcol + c * tk <= tpos)
        s = qk(q, k).reshape(NSA_HEADS, tq, tk)
        s = jnp.where(mask[None], s, NEG).reshape(r, tk)
        m_old = m_ref[...]
        m_new = jnp.maximum(m_old, jnp.max(s, axis=1, keepdims=True))
        alpha = jnp.exp(m_old - m_new)
        pp = jnp.exp(s - m_new)
        l_ref[...] = alpha * l_ref[...] + jnp.sum(pp, axis=1, keepdims=True)
        acc_ref[...] = alpha * acc_ref[...] + pv(pp.astype(BF16), k)
        m_ref[...] = m_new
        return carry

    lax.fori_loop(0, nchunks, step, 0)
    o_s = acc_ref[...] / l_ref[...]

    lw = wkv.shape[1] if keys_on_lanes else wkv.shape[0]
    wpos = wpos0 + lax.broadcasted_iota(jnp.int32, (1, lw), 1)
    dist = tpos - wpos
    mask_w = (dist >= 0) & (dist < WINDOW) & (wpos >= 0)
    s_w = qk(q, wkv).reshape(NSA_HEADS, tq, lw)
    s_w = jnp.where(mask_w[None], s_w, NEG).reshape(r, lw)
    m_w = jnp.max(s_w, axis=1, keepdims=True)
    p_w = jnp.exp(s_w - m_w)
    l_w = jnp.sum(p_w, axis=1, keepdims=True)
    o_w = pv(p_w.astype(BF16), wkv) / l_w

    def gcol(c):
        return jnp.concatenate([gate[:, 3 * h + c:3 * h + c + 1] for h in range(NSA_HEADS)], axis=0)

    return gcol(0) * o_c + gcol(1) * o_s + gcol(2) * o_w


def _nsa_prompt_body(q_ref, ckv_ref, skv_ref, wkv_ref, gate_ref, o_ref, m_ref, l_ref, acc_ref, *, tq, tk):
    qi = pl.program_id(1)
    r = NSA_HEADS * tq
    q = q_ref[...].reshape(r, LANE)
    tpos = qi * tq + lax.broadcasted_iota(jnp.int32, (tq, 1), 0)
    nchunks = (qi * tq + tq + tk - 1) // tk
    wkv = wkv_ref[0, pl.ds(pl.multiple_of(qi * tq, tq), WINDOW + tq), :]

    def get_chunk(c):
        return skv_ref[pl.ds(pl.multiple_of(c * tk, tk), tk), :]

    o = _nsa_core(q, tpos, ckv_ref[0], get_chunk, nchunks, tk, wkv, qi * tq - WINDOW, gate_ref[...],
                  m_ref, l_ref, acc_ref, tq=tq, n_cmp=SEQ // CMP_BLOCK, n_sel=SEQ // SEL_BLOCK, nsl=LANE,
                  keys_on_lanes=False)
    o_ref[...] = o.reshape(NSA_HEADS, tq, LANE)


def _nsa_prompt(nq, ckv, rowsb, wpad, gate, *, tq=128, tk=512):
    nqt = SEQ // tq
    r = NSA_HEADS * tq
    return pl.pallas_call(
        functools.partial(_nsa_prompt_body, tq=tq, tk=tk),
        grid=(BATCH, nqt),
        in_specs=[
            pl.BlockSpec((NSA_HEADS, tq, LANE), lambda b, i: (0, b * nqt + i, 0)),
            pl.BlockSpec((1, 2 * LANE, LANE), lambda b, i: (b, 0, 0)),
            pl.BlockSpec((SEQ, LANE), lambda b, i: (b, 1)),
            pl.BlockSpec((1, WINDOW + SEQ, LANE), lambda b, i: (b, 0, 0)),
            pl.BlockSpec((tq, LANE), lambda b, i: (b * nqt + i, 0)),
        ],
        out_specs=pl.BlockSpec((NSA_HEADS, tq, LANE), lambda b, i: (0, b * nqt + i, 0)),
        out_shape=jax.ShapeDtypeStruct((NSA_HEADS, MP, LANE), F32),
        scratch_shapes=[pltpu.VMEM((r, 1), F32), pltpu.VMEM((r, 1), F32), pltpu.VMEM((r, LANE), F32)],
        compiler_params=_params(("arbitrary", "arbitrary")),
        name="nsa_prompt",
    )(nq, ckv, rowsb, wpad, gate)


_SAMPLE_TK = 640
_SAMPLE_LK = PAST_LEN + PAGE_SIZE


def _nsa_sample_body(pt_ref, q_ref, ckv_ref, snew_ref, win_ref, wnew_ref, gate_ref, *rest):
    pages = rest[:N_PAGES]
    o_ref, skv_ref, wkv_ref, m_ref, l_ref, acc_ref = rest[N_PAGES:]
    tk = _SAMPLE_TK
    for i in range(N_PAGES + 1):
        c, off = divmod(i * PAGE_SIZE, tk)
        src = pages[i][...].astype(BF16) if i < N_PAGES else snew_ref[0]
        skv_ref[c, :, off:off + PAGE_SIZE] = src
    wkv_ref[:, 0:WINDOW] = win_ref[...].astype(BF16)
    wkv_ref[:, WINDOW:WINDOW + PAGE_SIZE] = wnew_ref[0]
    tpos = PAST_LEN + lax.broadcasted_iota(jnp.int32, (TS, 1), 0)
    n_sel = -(-(PAST_LEN + DEC_SEQ) // SEL_BLOCK)

    def get_chunk(c):
        return skv_ref[c]

    o = _nsa_core(q_ref[0], tpos, ckv_ref[0], get_chunk, _SAMPLE_LK // tk, tk,
                  wkv_ref[...], PAST_LEN - WINDOW, gate_ref[0],
                  m_ref, l_ref, acc_ref, tq=TS, n_cmp=PAST_LEN // CMP_BLOCK, n_sel=n_sel, nsl=2 * LANE,
                  keys_on_lanes=True)
    o_ref[0] = o


def _nsa_sample(page_flat, qs, ckv, snew_t, win_state_t, wnew_t, gates, cache_nsa_t, layer):
    r = NSA_HEADS * TS

    def page_spec(i):
        return pl.BlockSpec((None, None, LANE, PAGE_SIZE),
                            lambda b, pt: (layer, pt[b * N_PAGES + i], 1, 0))

    b3 = lambda b, pt: (b, 0, 0)
    grid_spec = pltpu.PrefetchScalarGridSpec(
        num_scalar_prefetch=1,
        grid=(DEC_BATCH,),
        in_specs=[
            pl.BlockSpec((1, r, LANE), b3),
            pl.BlockSpec((1, _NCL, LANE), b3),
            pl.BlockSpec((1, LANE, PAGE_SIZE), b3),
            pl.BlockSpec((None, None, LANE, WINDOW), lambda b, pt: (layer, b, 0, 0)),
            pl.BlockSpec((1, LANE, PAGE_SIZE), b3),
            pl.BlockSpec((1, TS, LANE), b3),
        ] + [page_spec(i) for i in range(N_PAGES)],
        out_specs=pl.BlockSpec((1, r, LANE), b3),
        scratch_shapes=[
            pltpu.VMEM((_SAMPLE_LK // _SAMPLE_TK, LANE, _SAMPLE_TK), BF16),
            pltpu.VMEM((LANE, WINDOW + PAGE_SIZE), BF16),
            pltpu.VMEM((r, 1), F32), pltpu.VMEM((r, 1), F32), pltpu.VMEM((r, LANE), F32),
        ],
    )
    return pl.pallas_call(
        _nsa_sample_body,
        grid_spec=grid_spec,
        out_shape=jax.ShapeDtypeStruct((DEC_BATCH, r, LANE), F32),
        compiler_params=_params(("arbitrary",)),
        name="nsa_sample",
    )(page_flat, qs, ckv, snew_t, win_state_t, wnew_t, gates, *([cache_nsa_t] * N_PAGES))


_CONV_PAD = 32
CONV_PAD_W = 32


def _conv_rows(xbuf, w, bias, g, beta, t_out):
    acc = jnp.zeros((t_out, CONV_CH), F32) + bias
    off = _CONV_PAD - (CONV_W - 1)
    for k in range(CONV_W):
        acc = acc + xbuf[off + k:off + k + t_out, :] * w[k:k + 1, :]
    mu = jnp.mean(acc, axis=-1, keepdims=True)
    var = jnp.mean(jnp.square(acc - mu), axis=-1, keepdims=True)
    y = (acc - mu) * lax.rsqrt(var + EPS) * g + beta
    return y * jax.nn.sigmoid(y)


def _conv_prompt_body(main_ref, halo_ref, w_ref, b_ref, g_ref, beta_ref, o_ref, *, tt):
    xbuf = jnp.concatenate([main_ref[0], halo_ref[0]], axis=0)
    o_ref[0] = _conv_rows(xbuf, w_ref[...], b_ref[...], g_ref[...], beta_ref[...], tt)


def _conv_prompt(buf, w, b, g, beta, *, tt=256):
    nt = SEQ // tt
    vec = lambda bb, j: (0, 0)
    return pl.pallas_call(
        functools.partial(_conv_prompt_body, tt=tt),
        grid=(BATCH, nt),
        in_specs=[
            pl.BlockSpec((1, tt, CONV_CH), lambda bb, j: (bb, j, 0)),
            pl.BlockSpec((1, _CONV_PAD, CONV_CH), lambda bb, j: (bb, (j + 1) * (tt // _CONV_PAD), 0)),
            pl.BlockSpec((CONV_PAD_W, CONV_CH), vec),
            pl.BlockSpec((1, CONV_CH), vec),
            pl.BlockSpec((1, CONV_CH), vec),
            pl.BlockSpec((1, CONV_CH), vec),
        ],
        out_specs=pl.BlockSpec((1, tt, CONV_CH), lambda bb, j: (bb, j, 0)),
        out_shape=jax.ShapeDtypeStruct((BATCH, SEQ, CONV_CH), F32),
        compiler_params=_params(("arbitrary", "arbitrary")),
        name="conv_prompt",
    )(buf, buf, w, b, g, beta)


def _conv_sample_body(buf_ref, w_ref, b_ref, g_ref, beta_ref, o_ref, *, nb):
    for i in range(nb):
        o_ref[i] = _conv_rows(buf_ref[i], w_ref[...], b_ref[...], g_ref[...], beta_ref[...], TS)


def _conv_sample(buf, w, b, g, beta, *, nb=16):
    rows = buf.shape[1]
    vec = lambda i: (0, 0)
    return pl.pallas_call(
        functools.partial(_conv_sample_body, nb=nb),
        grid=(DEC_BATCH // nb,),
        in_specs=[
            pl.BlockSpec((nb, rows, CONV_CH), lambda i: (i, 0, 0)),
            pl.BlockSpec((CONV_PAD_W, CONV_CH), vec),
            pl.BlockSpec((1, CONV_CH), vec),
            pl.BlockSpec((1, CONV_CH), vec),
            pl.BlockSpec((1, CONV_CH), vec),
        ],
        out_specs=pl.BlockSpec((nb, TS, CONV_CH), lambda i: (i, 0, 0)),
        out_shape=jax.ShapeDtypeStruct((DEC_BATCH, TS, CONV_CH), F32),
        compiler_params=_params(("arbitrary",)),
        name="conv_sample",
    )(buf, w, b, g, beta)


def _merge_body(x_ref, ol_ref, on_ref, oc_ref, wuv_ref, gn_ref, wo_ref, o_ref):
    om = jnp.concatenate(
        [_dot(ol_ref[:, h * MLA_KV_RANK:(h + 1) * MLA_KV_RANK], wuv_ref[h]) for h in range(MLA_HEADS)], axis=1)
    a1 = MLA_WIDTH
    a2 = MLA_WIDTH + NSA_WIDTH
    parts = jnp.concatenate([
        _rms(om, gn_ref[:, 0:a1]).astype(BF16),
        _rms(on_ref[...], gn_ref[:, a1:a2]).astype(BF16),
        _rms(oc_ref[...], gn_ref[:, a2:D_MODEL]).astype(BF16),
    ], axis=1)
    o_ref[...] = x_ref[...] + _dot(parts, wo_ref[...])


def _merge(x, o_lat, o_nsa, o_conv, wuv, gn, wo, *, tm=512):
    m = x.shape[0]
    row = lambda i: (i, 0)
    return pl.pallas_call(
        _merge_body,
        grid=(m // tm,),
        in_specs=[
            pl.BlockSpec((tm, D_MODEL), row),
            pl.BlockSpec((tm, MLA_HEADS * MLA_KV_RANK), row),
            pl.BlockSpec((tm, NSA_WIDTH), row),
            pl.BlockSpec((tm, CONV_CH), row),
            pl.BlockSpec((MLA_HEADS, MLA_KV_RANK, MLA_V), lambda i: (0, 0, 0)),
            pl.BlockSpec((1, D_MODEL), lambda i: (0, 0)),
            pl.BlockSpec((D_MODEL, D_MODEL), lambda i: (0, 0)),
        ],
        out_specs=pl.BlockSpec((tm, D_MODEL), row),
        out_shape=jax.ShapeDtypeStruct((m, D_MODEL), F32),
        compiler_params=_params(("arbitrary",)),
        name="merge",
    )(x, o_lat, o_nsa, o_conv, wuv, gn, wo)


def _mlp_body(x_ref, g_ref, w1_ref, w2_ref, o_ref, xn_ref, acc_ref):
    j = pl.program_id(1)

    @pl.when(j == 0)
    def _():
        xn_ref[...] = _rms(x_ref[...], g_ref[...]).astype(BF16)
        acc_ref[...] = jnp.zeros_like(acc_ref)

    h = jnp.maximum(_dot(xn_ref[...], w1_ref[...]), 0.0)
    acc_ref[...] += _dot((h * h).astype(BF16), w2_ref[...])

    @pl.when(j == pl.num_programs(1) - 1)
    def _():
        o_ref[...] = x_ref[...] + acc_ref[...]


def _mlp(x, g, w1, w2, *, tm=512, tf=512):
    m = x.shape[0]
    return pl.pallas_call(
        _mlp_body,
        grid=(m // tm, D_FF // tf),
        in_specs=[
            pl.BlockSpec((tm, D_MODEL), lambda i, j: (i, 0)),
            pl.BlockSpec((1, D_MODEL), lambda i, j: (0, 0)),
            pl.BlockSpec((D_MODEL, tf), lambda i, j: (0, j)),
            pl.BlockSpec((tf, D_MODEL), lambda i, j: (j, 0)),
        ],
        out_specs=pl.BlockSpec((tm, D_MODEL), lambda i, j: (i, 0)),
        out_shape=jax.ShapeDtypeStruct((m, D_MODEL), F32),
        scratch_shapes=[pltpu.VMEM((tm, D_MODEL), BF16), pltpu.VMEM((tm, D_MODEL), F32)],
        compiler_params=_params(("arbitrary", "arbitrary")),
        name="mlp",
    )(x, g, w1, w2)


def _final_norm_body(x_ref, g_ref, o_ref):
    o_ref[...] = _rms(x_ref[...], g_ref[...])


def _final_norm(x, g, *, tm=512):
    m = x.shape[0]
    return pl.pallas_call(
        _final_norm_body,
        grid=(m // tm,),
        in_specs=[pl.BlockSpec((tm, D_MODEL), lambda i: (i, 0)), pl.BlockSpec((1, D_MODEL), lambda i: (0, 0))],
        out_specs=pl.BlockSpec((tm, D_MODEL), lambda i: (i, 0)),
        out_shape=jax.ShapeDtypeStruct((m, D_MODEL), F32),
        compiler_params=_params(("arbitrary",)),
        name="final_norm",
    )(x, g)


def _pad_cols(w, n):
    return jnp.pad(w, ((0, 0), (0, n - w.shape[1])))


def _pack_w_in(w):
    segs = [
        w[:, OFF_CQ:OFF_CKV],
        w[:, OFF_CKV:OFF_KR],
        _pad_cols(w[:, OFF_KR:OFF_NQ], LANE),
        w[:, OFF_NQ:OFF_NKV],
        w[:, OFF_NKV:OFF_GATE],
        _pad_cols(w[:, OFF_GATE:OFF_CONV], LANE),
        w[:, OFF_CONV:IN_COLS],
    ]
    return jnp.concatenate(segs, axis=1).astype(BF16)


def _pack_w_q_up(w):
    w = w.reshape(MLA_Q_RANK, MLA_HEADS, MLA_NOPE + MLA_ROPE)
    nope = w[:, :, :MLA_NOPE].reshape(MLA_Q_RANK, MLA_HEADS * MLA_NOPE)
    rope = w[:, :, MLA_NOPE:].reshape(MLA_Q_RANK, MLA_HEADS * MLA_ROPE)
    return jnp.concatenate([nope, rope], axis=1).astype(BF16)


def _pack_cmp(pe, w1, w2):
    pe_t = jnp.concatenate([pe[0], pe[1]], axis=1)
    w1k = w1[0].reshape(CMP_BLOCK, NSA_HD, CMP_HIDDEN)
    w1v = w1[1].reshape(CMP_BLOCK, NSA_HD, CMP_HIDDEN)
    zero = jnp.zeros_like(w1k)
    w1b = jnp.concatenate([jnp.concatenate([w1k, zero], axis=2),
                           jnp.concatenate([zero, w1v], axis=2)], axis=1)
    z2 = jnp.zeros_like(w2[0])
    w2b = jnp.concatenate([jnp.concatenate([w2[0], z2], axis=1),
                           jnp.concatenate([z2, w2[1]], axis=1)], axis=0)
    return pe_t, w1b.astype(BF16), w2b.astype(BF16)


def _rope_tables():
    half = NSA_HD // 2
    freq = ROPE_THETA ** (-jnp.arange(half, dtype=F32) / half)
    pos = jnp.concatenate([jnp.arange(SEQ, dtype=jnp.int32),
                           PAST_LEN + (jnp.arange(256, dtype=jnp.int32) % DEC_SEQ)])
    ang = pos.astype(F32)[:, None] * freq[None, :]
    cos = jnp.tile(jnp.cos(ang), (1, LANE // half))
    sin = jnp.tile(jnp.sin(ang), (1, LANE // half))
    return cos, sin


def kernel(x_prompt, x_sample, cache_mla, cache_nsa, state_nsa_win, state_conv, page_table, norm_mix, w_in, mla_q_norm, mla_kv_norm, mla_w_q_up, mla_w_uk, mla_w_uv, nsa_cmp_pe, nsa_cmp_w1, nsa_cmp_w2, conv_w, conv_b, conv_ln_g, conv_ln_b, group_norm, w_out, norm_mlp, mlp_w1, mlp_w2, norm_final):
    x = jnp.concatenate([x_prompt.reshape(MP, D_MODEL), x_sample.reshape(MS, D_MODEL)], axis=0)
    cos_t, sin_t = _rope_tables()
    page_flat = page_table.reshape(-1).astype(jnp.int32)
    n_pool = cache_nsa.shape[1]
    cache_mla_t = jnp.swapaxes(cache_mla, 2, 3)
    cache_nsa_t = jnp.transpose(cache_nsa, (0, 1, 3, 4, 5, 2)).reshape(DEPTH, n_pool, 4 * NSA_HD, PAGE_SIZE)
    win_t = jnp.transpose(state_nsa_win, (0, 1, 3, 4, 5, 2)).reshape(DEPTH, DEC_BATCH, 2 * NSA_HD, WINDOW)
    perm, peo_p = _perm_matrices(SEQ // CMP_BLOCK)
    _, peo_s = _perm_matrices(PAST_LEN // CMP_BLOCK)

    mla_p, mla_s, nsa_p, nsa_s, win_p, win_s, conv_p, conv_s = [], [], [], [], [], [], [], []
    for l in range(DEPTH):
        wp = _pack_w_in(w_in[l])
        wq = _pack_w_q_up(mla_w_q_up[l])
        wuk = jnp.transpose(mla_w_uk[l], (1, 2, 0)).astype(BF16)
        wuv = jnp.transpose(mla_w_uv[l], (1, 0, 2)).astype(BF16)
        pe_t, w1b, w2b = _pack_cmp(nsa_cmp_pe[l], nsa_cmp_w1[l], nsa_cmp_w2[l])
        cw = jnp.pad(conv_w[l], ((0, CONV_PAD_W - CONV_W), (0, 0)))

        z = _inproj(x, norm_mix[l][None], wp)
        q384, kvb, mla_row, nq, rows, rowsb, gate, u = _post(
            z, cos_t, sin_t, mla_q_norm[l][None], mla_kv_norm[l][None], wq, wuk)

        mla_p.append(mla_row[:MP, :MLA_ROW].reshape(BATCH, SEQ, MLA_ROW))
        mla_s.append(mla_row[MP:, :MLA_ROW].reshape(DEC_BATCH, DEC_SEQ, MLA_ROW))
        nsa_p.append(rows[:MP, :4 * NSA_HD].reshape(BATCH, SEQ, 4, 1, NSA_HD))
        nsa_s.append(rows[MP:, :4 * NSA_HD].reshape(DEC_BATCH, DEC_SEQ, 4, 1, NSA_HD))
        win_rows_p = rows[:MP, 4 * NSA_HD:].reshape(BATCH, SEQ, 2, 1, NSA_HD)
        win_p.append(win_rows_p[:, SEQ - WINDOW:])
        win_rows_s = rows[MP:, 4 * NSA_HD:].reshape(DEC_BATCH, DEC_SEQ, 2, 1, NSA_HD)
        win_s.append(jnp.concatenate([state_nsa_win[l][:, DEC_SEQ:], win_rows_s], axis=1))
        up = u[:MP].reshape(BATCH, SEQ, CONV_CH)
        us = u[MP:].reshape(DEC_BATCH, DEC_SEQ, CONV_CH)
        bufs = jnp.concatenate([state_conv[l], us], axis=1)
        conv_p.append(up[:, SEQ - (CONV_W - 1):])
        conv_s.append(bufs[:, DEC_SEQ:])

        ol_p = _mla_prompt(q384, kvb)
        qs = q384[:, MP:].reshape(MLA_HEADS, DEC_BATCH, DEC_SEQ, KV_PAD)
        qs = jnp.transpose(qs, (1, 0, 2, 3)).reshape(DEC_BATCH, MLA_HEADS * DEC_SEQ, KV_PAD)
        knew = jnp.pad(kvb[MP:].reshape(DEC_BATCH, DEC_SEQ, KV_PAD), ((0, 0), (0, PAGE_SIZE - DEC_SEQ), (0, 0)))
        ol_s = _mla_sample(page_flat, qs, jnp.swapaxes(knew, 1, 2), cache_mla_t, l)
        ol_s = ol_s.reshape(DEC_BATCH, MLA_HEADS, DEC_SEQ, MLA_KV_RANK)
        ol_s = jnp.transpose(ol_s, (0, 2, 1, 3)).reshape(MS, MLA_HEADS * MLA_KV_RANK)
        o_lat = jnp.concatenate([ol_p, ol_s], axis=0)

        ckv_p = _compress_prompt(rows, perm, pe_t, w1b, w2b, peo_p)
        ckv_s = _compress_sample(page_flat, perm, pe_t, w1b, w2b, peo_s, cache_nsa_t, l)
        wpad = jnp.pad(rowsb[:MP, 2 * LANE:].reshape(BATCH, SEQ, LANE), ((0, 0), (WINDOW, 0), (0, 0)))
        on_p = _nsa_prompt(nq, ckv_p, rowsb, wpad, gate)
        on_p = jnp.transpose(on_p[:, :, NSA_HD:], (1, 0, 2)).reshape(MP, NSA_WIDTH)

        pad_tok = ((0, 0), (0, 0), (0, TS - DEC_SEQ), (0, 0))
        nqs = jnp.pad(nq[:, MP:].reshape(NSA_HEADS, DEC_BATCH, DEC_SEQ, LANE), pad_tok)
        nqs = jnp.transpose(nqs, (1, 0, 2, 3)).reshape(DEC_BATCH, NSA_HEADS * TS, LANE)
        pad_rows = ((0, 0), (0, PAGE_SIZE - DEC_SEQ), (0, 0))
        snew = jnp.pad(rowsb[MP:, LANE:2 * LANE].reshape(DEC_BATCH, DEC_SEQ, LANE), pad_rows)
        wnew = jnp.pad(rowsb[MP:, 2 * LANE:].reshape(DEC_BATCH, DEC_SEQ, LANE), pad_rows)
        gs = jnp.pad(gate[MP:].reshape(DEC_BATCH, DEC_SEQ, LANE), ((0, 0), (0, TS - DEC_SEQ), (0, 0)))
        on_s = _nsa_sample(page_flat, nqs, ckv_s, jnp.swapaxes(snew, 1, 2), win_t, jnp.swapaxes(wnew, 1, 2),
                           gs, cache_nsa_t, l)
        on_s = on_s[:, :, NSA_HD:].reshape(DEC_BATCH, NSA_HEADS, TS, NSA_HD)[:, :, :DEC_SEQ]
        on_s = jnp.transpose(on_s, (0, 2, 1, 3)).reshape(MS, NSA_WIDTH)
        o_nsa = jnp.concatenate([on_p, on_s], axis=0)

        bufp = jnp.pad(up, ((0, 0), (_CONV_PAD, 0), (0, 0)))
        oc_p = _conv_prompt(bufp, cw, conv_b[l][None], conv_ln_g[l][None], conv_ln_b[l][None])
        bufs_p = jnp.pad(bufs, ((0, 0), (_CONV_PAD - (CONV_W - 1), TS - DEC_SEQ), (0, 0)))
        oc_s = _conv_sample(bufs_p, cw, conv_b[l][None], conv_ln_g[l][None], conv_ln_b[l][None])
        o_conv = jnp.concatenate([oc_p.reshape(MP, CONV_CH), oc_s[:, :DEC_SEQ].reshape(MS, CONV_CH)], axis=0)

        x = _merge(x, o_lat, o_nsa, o_conv, wuv, group_norm[l][None], w_out[l].astype(BF16))
        x = _mlp(x, norm_mlp[l][None], mlp_w1[l].astype(BF16), mlp_w2[l].astype(BF16))

    y = _final_norm(x, norm_final[None])
    return (y[:MP].reshape(BATCH, SEQ, D_MODEL), y[MP:].reshape(DEC_BATCH, DEC_SEQ, D_MODEL),
            jnp.stack(mla_p), jnp.stack(mla_s), jnp.stack(nsa_p), jnp.stack(nsa_s),
            jnp.stack(win_p), jnp.stack(win_s), jnp.stack(conv_p), jnp.stack(conv_s))
```

```python
import functools
import math

import jax
import jax.numpy as jnp
from jax import lax
from jax.experimental import pallas as pl
from jax.experimental.pallas import tpu as pltpu

F32 = jnp.float32
BF16 = jnp.bfloat16

D_MODEL = 2048
BATCH = 4
SEQ = 2048
DEPTH = 4
DEC_BATCH = 128
DEC_SEQ = 4
PAST_LEN = 8192
PAGE_SIZE = 128
N_PAGES = PAST_LEN // PAGE_SIZE
MLA_NOPE = 128
MLA_ROPE = 64
MLA_V = 128
MLA_HEADS = 6
MLA_Q_RANK = 512
MLA_KV_RANK = 256
MLA_ROW = MLA_KV_RANK + MLA_ROPE
MLA_WIDTH = MLA_HEADS * MLA_V
MLA_SCALE = 1.0 / math.sqrt(MLA_NOPE + MLA_ROPE)
NSA_HD = 64
NSA_HEADS = 12
NSA_WIDTH = NSA_HEADS * NSA_HD
NSA_SCALE = 1.0 / math.sqrt(NSA_HD)
CMP_BLOCK = 32
CMP_HIDDEN = 128
SEL_BLOCK = 64
SEL_TOPK = 16
WINDOW = 512
SEL_FORCE = 1e4
CONV_CH = 512
CONV_W = 31
D_FF = 4 * D_MODEL
ROPE_THETA = 10000.0
EPS = 1e-6

OFF_CQ = 0
OFF_CKV = OFF_CQ + MLA_Q_RANK
OFF_KR = OFF_CKV + MLA_KV_RANK
OFF_NQ = OFF_KR + MLA_ROPE
OFF_NKV = OFF_NQ + NSA_WIDTH
OFF_GATE = OFF_NKV + 6 * NSA_HD
OFF_CONV = OFF_GATE + 3 * NSA_HEADS
IN_COLS = OFF_CONV + 2 * CONV_CH

LANE = 128
VMEM_LIMIT = 56 * 1024 * 1024

MP = BATCH * SEQ
MS = DEC_BATCH * DEC_SEQ
MT = MP + MS
TS = 8
KV_PAD = 3 * LANE

Z_CQ = 0
Z_CKV = 512
Z_KR = 768
Z_NQ = 896
Z_NKV = 1664
Z_GATE = 2048
Z_CA = 2176
Z_CB = 2688
NZ = 3200

NEG = -1e30


def _dot(a, b):
    return jnp.dot(a, b, preferred_element_type=F32)


def _dot_nt(a, b):
    return lax.dot_general(a, b, (((1,), (1,)), ((), ())), preferred_element_type=F32)


def _rms(x, g):
    return x * lax.rsqrt(jnp.mean(x * x, axis=-1, keepdims=True) + EPS) * g


def _rep(x, n):
    return x if n == 1 else jnp.concatenate([x] * n, axis=1)


def _params(sem):
    return pltpu.CompilerParams(dimension_semantics=sem, vmem_limit_bytes=VMEM_LIMIT)


def _inproj_body(x_ref, g_ref, w_ref, o_ref, xn_ref):
    @pl.when(pl.program_id(1) == 0)
    def _():
        xn_ref[...] = _rms(x_ref[...], g_ref[...]).astype(BF16)

    o_ref[...] = _dot_nt(xn_ref[...], w_ref[...])


def _inproj(x, g, w_t, *, tm=512, tn=640):
    m = x.shape[0]
    return pl.pallas_call(
        _inproj_body,
        grid=(m // tm, NZ // tn),
        in_specs=[
            pl.BlockSpec((tm, D_MODEL), lambda i, j: (i, 0)),
            pl.BlockSpec((1, D_MODEL), lambda i, j: (0, 0)),
            pl.BlockSpec((tn, D_MODEL), lambda i, j: (j, 0)),
        ],
        out_specs=pl.BlockSpec((tm, tn), lambda i, j: (i, j)),
        out_shape=jax.ShapeDtypeStruct((m, NZ), F32),
        scratch_shapes=[pltpu.VMEM((tm, D_MODEL), BF16)],
        compiler_params=_params(("arbitrary", "arbitrary")),
        name="inproj",
    )(x, g, w_t)


def _post_body(z_ref, cos_ref, sin_ref, qn_ref, kvn_ref, wq_ref, wuk_ref,
               q_ref, kvb_ref, mla_ref, nq_ref, rows_ref, rowsb_ref, gate_ref, u_ref, *, tm):
    cos = cos_ref[...]
    sin = sin_ref[...]
    lane = lax.broadcasted_iota(jnp.int32, (tm, LANE), 1)
    first = (lane & 63) < 32
    lo = lane < 64
    sa = jnp.where(first, -sin, 0.0)
    sb = jnp.where(first, 0.0, sin)
    ch = jnp.where(lo, cos, 1.0)
    sah = jnp.where(lo, sa, 0.0)
    sbh = jnp.where(lo, sb, 0.0)

    def rope_full(x):
        return x * cos + pltpu.roll(x, 96, 1) * sa + pltpu.roll(x, 32, 1) * sb

    def rope_half(x):
        return x * ch + pltpu.roll(x, 96, 1) * sah + pltpu.roll(x, 32, 1) * sbh

    def split_heads(t):
        return jnp.where(lo, t, 0.0), jnp.where(lo, pltpu.roll(t, 64, 1), 0.0)

    cq = _rms(z_ref[:, Z_CQ:Z_CQ + MLA_Q_RANK], qn_ref[...]).astype(BF16)
    q = _dot(cq, wq_ref[...])
    for h in range(MLA_HEADS):
        ql = _dot(q[:, h * MLA_NOPE:(h + 1) * MLA_NOPE].astype(BF16), wuk_ref[h])
        q_ref[h, :, 0:MLA_KV_RANK] = (ql * MLA_SCALE).astype(BF16)
    rope0 = MLA_HEADS * MLA_NOPE
    for j in range(MLA_HEADS // 2):
        t = rope_full(q[:, rope0 + j * LANE:rope0 + (j + 1) * LANE]) * MLA_SCALE
        a, b = split_heads(t)
        q_ref[2 * j, :, MLA_KV_RANK:KV_PAD] = a.astype(BF16)
        q_ref[2 * j + 1, :, MLA_KV_RANK:KV_PAD] = b.astype(BF16)

    ckv = _rms(z_ref[:, Z_CKV:Z_CKV + MLA_KV_RANK], kvn_ref[...])
    kr = rope_half(z_ref[:, Z_KR:Z_KR + LANE])
    mla_ref[:, 0:MLA_KV_RANK] = ckv
    mla_ref[:, MLA_KV_RANK:KV_PAD] = kr
    kvb_ref[:, 0:MLA_KV_RANK] = ckv.astype(BF16)
    kvb_ref[:, MLA_KV_RANK:KV_PAD] = kr.astype(BF16)

    for j in range(NSA_HEADS // 2):
        t = rope_full(z_ref[:, Z_NQ + j * LANE:Z_NQ + (j + 1) * LANE]) * NSA_SCALE
        a, b = split_heads(t)
        nq_ref[2 * j] = a.astype(BF16)
        nq_ref[2 * j + 1] = b.astype(BF16)

    for j in range(3):
        t = rope_half(z_ref[:, Z_NKV + j * LANE:Z_NKV + (j + 1) * LANE])
        rows_ref[:, j * LANE:(j + 1) * LANE] = t
        rowsb_ref[:, j * LANE:(j + 1) * LANE] = t.astype(BF16)

    gate_ref[...] = jax.nn.sigmoid(z_ref[:, Z_GATE:Z_GATE + LANE])
    u_ref[...] = z_ref[:, Z_CA:Z_CA + CONV_CH] * jax.nn.sigmoid(z_ref[:, Z_CB:Z_CB + CONV_CH])


def _post(z, cos_t, sin_t, qn, kvn, wq, wuk, *, tm=256):
    m = z.shape[0]
    n_prompt_tiles = MP // tm
    tiles_per_seq = SEQ // tm

    def tab_map(i):
        return (jnp.where(i < n_prompt_tiles, i % tiles_per_seq, tiles_per_seq), 0)

    row = lambda i: (i, 0)
    const2 = lambda i: (0, 0)
    return pl.pallas_call(
        functools.partial(_post_body, tm=tm),
        grid=(m // tm,),
        in_specs=[
            pl.BlockSpec((tm, NZ), row),
            pl.BlockSpec((tm, LANE), tab_map),
            pl.BlockSpec((tm, LANE), tab_map),
            pl.BlockSpec((1, MLA_Q_RANK), const2),
            pl.BlockSpec((1, MLA_KV_RANK), const2),
            pl.BlockSpec((MLA_Q_RANK, MLA_HEADS * (MLA_NOPE + MLA_ROPE)), const2),
            pl.BlockSpec((MLA_HEADS, MLA_NOPE, MLA_KV_RANK), lambda i: (0, 0, 0)),
        ],
        out_specs=[
            pl.BlockSpec((MLA_HEADS, tm, KV_PAD), lambda i: (0, i, 0)),
            pl.BlockSpec((tm, KV_PAD), row),
            pl.BlockSpec((tm, KV_PAD), row),
            pl.BlockSpec((NSA_HEADS, tm, LANE), lambda i: (0, i, 0)),
            pl.BlockSpec((tm, 3 * LANE), row),
            pl.BlockSpec((tm, 3 * LANE), row),
            pl.BlockSpec((tm, LANE), row),
            pl.BlockSpec((tm, CONV_CH), row),
        ],
        out_shape=[
            jax.ShapeDtypeStruct((MLA_HEADS, m, KV_PAD), BF16),
            jax.ShapeDtypeStruct((m, KV_PAD), BF16),
            jax.ShapeDtypeStruct((m, KV_PAD), F32),
            jax.ShapeDtypeStruct((NSA_HEADS, m, LANE), BF16),
            jax.ShapeDtypeStruct((m, 3 * LANE), F32),
            jax.ShapeDtypeStruct((m, 3 * LANE), BF16),
            jax.ShapeDtypeStruct((m, LANE), F32),
            jax.ShapeDtypeStruct((m, CONV_CH), F32),
        ],
        compiler_params=_params(("arbitrary",)),
        name="post_proj",
    )(z, cos_t, sin_t, qn, kvn, wq, wuk)


def _mla_prompt_body(q_ref, kv_ref, o_ref, m_ref, l_ref, acc_ref, *, tq, tk, nsplit):
    qi = pl.program_id(1)
    r = MLA_HEADS * tq
    q = q_ref[...].reshape(r, KV_PAD)
    m_ref[...] = jnp.full((r, LANE), NEG, F32)
    l_ref[...] = jnp.zeros((r, LANE), F32)
    acc_ref[...] = jnp.zeros((r, MLA_KV_RANK), F32)
    rg = r // nsplit
    qpos = qi * tq + (lax.broadcasted_iota(jnp.int32, (rg, tk), 0) & (tq - 1))
    col = lax.broadcasted_iota(jnp.int32, (rg, tk), 1)
    nchunks = (qi * tq + tq + tk - 1) // tk

    def step(j, carry):
        k = kv_ref[pl.ds(pl.multiple_of(j * tk, tk), tk), :]
        for g in range(nsplit):
            rows = slice(g * rg, (g + 1) * rg)
            s = _dot_nt(q[rows], k)
            s = jnp.where(j * tk + col <= qpos, s, NEG)
            m_old = m_ref[rows, :]
            m_new = jnp.maximum(m_old, jnp.max(s, axis=1, keepdims=True))
            alpha = jnp.exp(m_old - m_new)
            p = jnp.exp(s - _rep(m_new, tk // LANE))
            l_ref[rows, :] = alpha * l_ref[rows, :] + jnp.sum(p, axis=1, keepdims=True)
            acc_ref[rows, :] = (_rep(alpha, MLA_KV_RANK // LANE) * acc_ref[rows, :]
                                + _dot(p.astype(BF16), k[:, 0:MLA_KV_RANK]))
            m_ref[rows, :] = m_new
        return carry

    lax.fori_loop(0, nchunks, step, 0)
    o = acc_ref[...] / _rep(l_ref[...], MLA_KV_RANK // LANE)
    for h in range(MLA_HEADS):
        o_ref[:, h * MLA_KV_RANK:(h + 1) * MLA_KV_RANK] = o[h * tq:(h + 1) * tq].astype(BF16)


def _mla_prompt(q, kvb, *, tq=256, tk=512, nsplit=2):
    nq = SEQ // tq
    r = MLA_HEADS * tq
    return pl.pallas_call(
        functools.partial(_mla_prompt_body, tq=tq, tk=tk, nsplit=nsplit),
        grid=(BATCH, nq),
        in_specs=[
            pl.BlockSpec((MLA_HEADS, tq, KV_PAD), lambda b, i: (0, b * nq + i, 0)),
            pl.BlockSpec((SEQ, KV_PAD), lambda b, i: (b, 0)),
        ],
        out_specs=pl.BlockSpec((tq, MLA_HEADS * MLA_KV_RANK), lambda b, i: (b * nq + i, 0)),
        out_shape=jax.ShapeDtypeStruct((MP, MLA_HEADS * MLA_KV_RANK), BF16),
        scratch_shapes=[pltpu.VMEM((r, LANE), F32), pltpu.VMEM((r, LANE), F32), pltpu.VMEM((r, MLA_KV_RANK), F32)],
        compiler_params=_params(("arbitrary", "arbitrary")),
        name="mla_prompt",
    )(q, kvb)


def _page_copy(cache_ref, pages_ref, sem_ref, layer, page, slot, i):
    return pltpu.make_async_copy(cache_ref.at[layer, page], pages_ref.at[slot, i], sem_ref.at[slot])


def _paged_prefetch(pt_ref, cache_ref, pages_ref, sem_ref, layer):
    b = pl.program_id(0)
    nb = pl.num_programs(0)
    slot = b & 1

    @pl.when(b == 0)
    def _():
        for i in range(N_PAGES):
            _page_copy(cache_ref, pages_ref, sem_ref, layer, pt_ref[i], 0, i).start()

    for i in range(N_PAGES):
        _page_copy(cache_ref, pages_ref, sem_ref, layer, 0, slot, i).wait()
    nxt = jnp.where(b + 1 == nb, 0, b + 1)
    for i in range(N_PAGES):
        _page_copy(cache_ref, pages_ref, sem_ref, layer, pt_ref[nxt * N_PAGES + i], 1 - slot, i).start()
    return slot


def _paged_drain(cache_ref, pages_ref, sem_ref, layer, slot):
    @pl.when(pl.program_id(0) == pl.num_programs(0) - 1)
    def _():
        for i in range(N_PAGES):
            _page_copy(cache_ref, pages_ref, sem_ref, layer, 0, 1 - slot, i).wait()


def _mla_sample_body(pt_ref, q_ref, knew_ref, cache_ref, o_ref, pages_ref, sem_ref, kt_ref, *, layer):
    slot = _paged_prefetch(pt_ref, cache_ref, pages_ref, sem_ref, layer)
    lk = PAST_LEN + PAGE_SIZE
    kt_ref[MLA_ROW:KV_PAD, :] = jnp.zeros((KV_PAD - MLA_ROW, lk), BF16)
    for i in range(N_PAGES):
        kt_ref[0:MLA_ROW, i * PAGE_SIZE:(i + 1) * PAGE_SIZE] = pages_ref[slot, i].astype(BF16)
    kt_ref[:, PAST_LEN:lk] = knew_ref[0]
    q = q_ref[0]
    r = q.shape[0]
    s = _dot(q, kt_ref[...])
    col = lax.broadcasted_iota(jnp.int32, (r, lk), 1)
    row = lax.broadcasted_iota(jnp.int32, (r, lk), 0)
    valid = (col < PAST_LEN) | ((col - PAST_LEN) <= (row & (DEC_SEQ - 1)))
    s = jnp.where(valid, s, NEG)
    m = jnp.max(s, axis=1, keepdims=True)
    p = jnp.exp(s - m)
    l = jnp.sum(p, axis=1, keepdims=True)
    o = _dot_nt(p.astype(BF16), kt_ref[0:MLA_KV_RANK, :]) / l
    o_ref[0] = o.astype(BF16)
    _paged_drain(cache_ref, pages_ref, sem_ref, layer, slot)


def _mla_sample(page_flat, qs, knew_t, cache_mla_t, layer):
    r = MLA_HEADS * DEC_SEQ
    grid_spec = pltpu.PrefetchScalarGridSpec(
        num_scalar_prefetch=1,
        grid=(DEC_BATCH,),
        in_specs=[
            pl.BlockSpec((1, r, KV_PAD), lambda b, pt: (b, 0, 0)),
            pl.BlockSpec((1, KV_PAD, PAGE_SIZE), lambda b, pt: (b, 0, 0)),
            pl.BlockSpec(memory_space=pl.ANY),
        ],
        out_specs=pl.BlockSpec((1, r, MLA_KV_RANK), lambda b, pt: (b, 0, 0)),
        scratch_shapes=[
            pltpu.VMEM((2, N_PAGES, MLA_ROW, PAGE_SIZE), F32),
            pltpu.SemaphoreType.DMA((2,)),
            pltpu.VMEM((KV_PAD, PAST_LEN + PAGE_SIZE), BF16),
        ],
    )
    return pl.pallas_call(
        functools.partial(_mla_sample_body, layer=layer),
        grid_spec=grid_spec,
        out_shape=jax.ShapeDtypeStruct((DEC_BATCH, r, MLA_KV_RANK), BF16),
        compiler_params=_params(("arbitrary",)),
        name="mla_sample",
    )(page_flat, qs, knew_t, cache_mla_t)


def _gelu_tanh(x):
    return 0.5 * x * (1.0 + jnp.tanh(math.sqrt(2.0 / math.pi) * (x + 0.044715 * (x * x * x))))


_PAIR = 2 * PAGE_SIZE
_BPP = _PAIR // CMP_BLOCK
_NCL = 2 * LANE


def _scatter_pair(gall, g_ref, pp):
    for r in range(CMP_BLOCK):
        g_ref[r // 2, pp * _BPP:(pp + 1) * _BPP, (r % 2) * LANE:(r % 2 + 1) * LANE] = gall[r * _BPP:(r + 1) * _BPP, :]


def _compress_finish(g_ref, pe_ref, w1_ref, w2_ref, peo_ref, *, nb):
    nslab = CMP_BLOCK // 2
    for q in range(nslab):
        g_ref[q, nb:nb + 8, :] = jnp.broadcast_to(pe_ref[q:q + 1, :], (8, 2 * LANE))
    h0 = jnp.zeros((nb + 8, 2 * CMP_HIDDEN), F32)
    h1 = jnp.zeros((nb + 8, 2 * CMP_HIDDEN), F32)
    for q in range(0, nslab, 2):
        h0 = h0 + _dot(g_ref[q].astype(BF16), w1_ref[q])
        h1 = h1 + _dot(g_ref[q + 1].astype(BF16), w1_ref[q + 1])
    h = h0 + h1
    hb = h[0:nb] + h[nb:nb + 1]
    c = _dot(_gelu_tanh(hb).astype(BF16), w2_ref[...]).astype(BF16)
    return _dot(peo_ref[...], c).astype(BF16)


def _compress_prompt_body(src_ref, perm_ref, pe_ref, w1_ref, w2_ref, peo_ref, o_ref, g_ref):
    for pp in range(SEQ // _PAIR):
        x = src_ref[pp * _PAIR:(pp + 1) * _PAIR, :].astype(BF16)
        _scatter_pair(_dot(perm_ref[...], x), g_ref, pp)
    o_ref[0] = _compress_finish(g_ref, pe_ref, w1_ref, w2_ref, peo_ref, nb=SEQ // CMP_BLOCK)


def _compress_prompt(rows, perm, pe, w1, w2, peo):
    nb = SEQ // CMP_BLOCK
    c2 = lambda b: (0, 0)
    return pl.pallas_call(
        _compress_prompt_body,
        grid=(BATCH,),
        in_specs=[
            pl.BlockSpec((SEQ, LANE), lambda b: (b, 0)),
            pl.BlockSpec((_PAIR, _PAIR), c2),
            pl.BlockSpec((CMP_BLOCK // 2, 2 * LANE), c2),
            pl.BlockSpec((CMP_BLOCK // 2, 2 * LANE, 2 * CMP_HIDDEN), lambda b: (0, 0, 0)),
            pl.BlockSpec((2 * CMP_HIDDEN, LANE), c2),
            pl.BlockSpec((_NCL, nb), c2),
        ],
        out_specs=pl.BlockSpec((1, _NCL, LANE), lambda b: (b, 0, 0)),
        out_shape=jax.ShapeDtypeStruct((BATCH, _NCL, LANE), BF16),
        scratch_shapes=[pltpu.VMEM((CMP_BLOCK // 2, nb + 8, 2 * LANE), F32)],
        compiler_params=_params(("arbitrary",)),
        name="compress_prompt",
    )(rows, perm, pe, w1, w2, peo)


def _perm_matrices(nb):
    i = jnp.arange(_PAIR)
    perm = (jnp.arange(_PAIR)[None, :] == (CMP_BLOCK * (i % _BPP) + i // _BPP)[:, None]).astype(BF16)
    j = jnp.arange(_NCL)
    blk = jnp.where(j < _NCL // 2, 2 * j, 2 * (j - _NCL // 2) + 1)
    peo = (jnp.arange(nb)[None, :] == blk[:, None]).astype(BF16)
    return perm, peo


def _topk_mask_lanes(score, n_sel, k):
    tq, nsl = score.shape
    mm = lax.broadcasted_iota(jnp.int32, (tq, nsl), 1)
    rank = jnp.zeros((tq, nsl), F32)
    for j in range(n_sel):
        sc = score[:, j:j + 1]
        ahead = (sc > score) | ((sc == score) & (mm > j))
        rank = rank + jnp.where(ahead, 1.0, 0.0)
    return jnp.where(rank < float(k), 1.0, 0.0)


def _topk_mask_sublanes(score, n_sel, k):
    st = score.T
    nsl, tq = st.shape
    mm = lax.broadcasted_iota(jnp.int32, (nsl, tq), 0)
    rank = jnp.zeros((nsl, tq), F32)
    for j in range(n_sel):
        sc = st[j:j + 1, :]
        ahead = (sc > st) | ((sc == st) & (mm > j))
        rank = rank + jnp.where(ahead, 1.0, 0.0)
    return jnp.where(rank < float(k), 1.0, 0.0).T


def _nsa_core(q, tpos, ckv, get_chunk, get_expand, nchunks, tk, wkv, wpos0, gate,
              m_ref, l_ref, acc_ref, *, tq, n_cmp, n_sel, nsl, keys_on_lanes, nsplit):
    if keys_on_lanes:
        def qk(a, kv):
            return _dot(a[:, 0:NSA_HD], kv[0:NSA_HD])

        def pv(p, kv):
            o = _dot_nt(p, kv[NSA_HD:2 * NSA_HD])
            return jnp.concatenate([jnp.zeros_like(o), o], axis=1)
    else:
        qk = _dot_nt
        pv = _dot
    r = NSA_HEADS * tq
    qpos = jnp.concatenate([tpos] * NSA_HEADS, axis=0)

    ncl = ckv.shape[0]
    half = ncl // 2
    s_c = _dot_nt(q, ckv)
    jc = lax.broadcasted_iota(jnp.int32, (1, ncl), 1)
    blk = jnp.where(jc < half, 2 * jc, 2 * (jc - half) + 1)
    valid_c = (blk < n_cmp) & (blk * CMP_BLOCK + (CMP_BLOCK - 1) <= qpos)
    s_c = jnp.where(valid_c, s_c, NEG)
    m_c = jnp.broadcast_to(jnp.max(s_c, axis=1, keepdims=True), (r, LANE))
    p = jnp.where(valid_c, jnp.exp(s_c - _rep(m_c, ncl // LANE)), 0.0)
    d = jnp.broadcast_to(jnp.sum(p, axis=1, keepdims=True), (r, LANE))
    p_c = p / _rep(jnp.where(d > 0, d, 1.0), ncl // LANE)
    o_c = _dot(p_c.astype(BF16), ckv)

    imp = p_c[0:tq]
    for h in range(1, NSA_HEADS):
        imp = imp + p_c[h * tq:(h + 1) * tq]
    imp_sel = imp[:, 0:half] + imp[:, half:ncl]
    if nsl > half:
        imp_sel = jnp.concatenate([imp_sel, jnp.zeros((tq, nsl - half), F32)], axis=1)
    mm = lax.broadcasted_iota(jnp.int32, (tq, nsl), 1)
    cur = tpos >> 6
    valid_s = (mm * SEL_BLOCK <= tpos) & (mm < n_sel)
    forced = (mm == 0) | (mm == cur) | (mm == cur - 1)
    score = jnp.where(valid_s, jnp.where(forced, SEL_FORCE, imp_sel), -jnp.inf)
    topk = min(SEL_TOPK, n_sel)
    if tq == nsl:
        selb = _topk_mask_sublanes(score, n_sel, topk).astype(BF16)
    else:
        selb = _topk_mask_lanes(score, n_sel, topk).astype(BF16)

    hg = NSA_HEADS // nsplit
    rg = hg * tq
    kcol = lax.broadcasted_iota(jnp.int32, (tq, tk), 1)

    def step(c, first):
        k = get_chunk(c)
        chosen = _dot(selb, get_expand(c)) > 0.5
        mask = (chosen & (kcol + c * tk <= tpos))[None]
        for g in range(nsplit):
            rows = slice(g * rg, (g + 1) * rg)
            s = qk(q[rows], k).reshape(hg, tq, tk)
            s = jnp.where(mask, s, NEG).reshape(rg, tk)
            m_cur = jnp.max(s, axis=1, keepdims=True)
            if first:
                m_new = jnp.broadcast_to(m_cur, (rg, LANE))
            else:
                m_old = m_ref[rows, :]
                m_new = jnp.maximum(m_old, m_cur)
                alpha = jnp.exp(m_old - m_new)
            pp = jnp.exp(s - _rep(m_new, tk // LANE))
            psum = jnp.sum(pp, axis=1, keepdims=True)
            acc = pv(pp.astype(BF16), k)
            if first:
                l_ref[rows, :] = jnp.broadcast_to(psum, (rg, LANE))
                acc_ref[rows, :] = acc
            else:
                l_ref[rows, :] = alpha * l_ref[rows, :] + psum
                acc_ref[rows, :] = alpha * acc_ref[rows, :] + acc
            m_ref[rows, :] = m_new

    step(0, True)
    if not (isinstance(nchunks, int) and nchunks == 1):
        def body(c, carry):
            step(c, False)
            return carry
        lax.fori_loop(1, nchunks, body, 0)
    o_s = acc_ref[...] / l_ref[...]

    lw = wkv.shape[1] if keys_on_lanes else wkv.shape[0]
    wpos = wpos0 + lax.broadcasted_iota(jnp.int32, (1, lw), 1)
    dist = tpos - wpos
    mask_w = (dist >= 0) & (dist < WINDOW) & (wpos >= 0)
    o_groups = []
    for g in range(nsplit):
        s_w = qk(q[g * rg:(g + 1) * rg], wkv).reshape(hg, tq, lw)
        s_w = jnp.where(mask_w[None], s_w, NEG).reshape(rg, lw)
        m_w = jnp.broadcast_to(jnp.max(s_w, axis=1, keepdims=True), (rg, LANE))
        p_w = jnp.exp(s_w - _rep(m_w, lw // LANE))
        l_w = jnp.broadcast_to(jnp.sum(p_w, axis=1, keepdims=True), (rg, LANE))
        o_groups.append(pv(p_w.astype(BF16), wkv) / l_w)
    o_w = o_groups[0] if nsplit == 1 else jnp.concatenate(o_groups, axis=0)

    def gcol(c):
        return jnp.concatenate([gate[:, 3 * h + c:3 * h + c + 1] for h in range(NSA_HEADS)], axis=0)

    return gcol(0) * o_c + gcol(1) * o_s + gcol(2) * o_w


def _nsa_prompt_body(q_ref, ckv_ref, skv_ref, wkv_ref, gate_ref, o_ref, m_ref, l_ref, acc_ref, *, tq, tk, nsplit):
    qi = pl.program_id(1)
    r = NSA_HEADS * tq
    q = q_ref[...].reshape(r, LANE)
    tpos = qi * tq + lax.broadcasted_iota(jnp.int32, (tq, 1), 0)
    nchunks = (qi * tq + tq + tk - 1) // tk
    wkv = wkv_ref[0, pl.ds(pl.multiple_of(qi * tq, tq), WINDOW + tq), :]

    def get_chunk(c):
        return skv_ref[pl.ds(pl.multiple_of(c * tk, tk), tk), :]

    bpc = tk // SEL_BLOCK
    em = lax.broadcasted_iota(jnp.int32, (LANE, tk), 0)
    ek = lax.broadcasted_iota(jnp.int32, (LANE, tk), 1) >> 6

    def get_expand(c):
        return jnp.where(em == ek + c * bpc, 1.0, 0.0).astype(BF16)

    o = _nsa_core(q, tpos, ckv_ref[0], get_chunk, get_expand, nchunks, tk, wkv, qi * tq - WINDOW, gate_ref[...],
                  m_ref, l_ref, acc_ref, tq=tq, n_cmp=SEQ // CMP_BLOCK, n_sel=SEQ // SEL_BLOCK, nsl=LANE,
                  keys_on_lanes=False, nsplit=nsplit)
    o_ref[...] = o.reshape(NSA_HEADS, tq, LANE)


def _nsa_prompt(nq, ckv, rowsb, wpad, gate, *, tq=128, tk=512, nsplit=2):
    nqt = SEQ // tq
    r = NSA_HEADS * tq
    return pl.pallas_call(
        functools.partial(_nsa_prompt_body, tq=tq, tk=tk, nsplit=nsplit),
        grid=(BATCH, nqt),
        in_specs=[
            pl.BlockSpec((NSA_HEADS, tq, LANE), lambda b, i: (0, b * nqt + i, 0)),
            pl.BlockSpec((1, 2 * LANE, LANE), lambda b, i: (b, 0, 0)),
            pl.BlockSpec((SEQ, LANE), lambda b, i: (b, 1)),
            pl.BlockSpec((1, WINDOW + SEQ, LANE), lambda b, i: (b, 0, 0)),
            pl.BlockSpec((tq, LANE), lambda b, i: (b * nqt + i, 0)),
        ],
        out_specs=pl.BlockSpec((NSA_HEADS, tq, LANE), lambda b, i: (0, b * nqt + i, 0)),
        out_shape=jax.ShapeDtypeStruct((NSA_HEADS, MP, LANE), F32),
        scratch_shapes=[pltpu.VMEM((r, LANE), F32), pltpu.VMEM((r, LANE), F32), pltpu.VMEM((r, LANE), F32)],
        compiler_params=_params(("arbitrary", "arbitrary")),
        name="nsa_prompt",
    )(nq, ckv, rowsb, wpad, gate)


_SAMPLE_LK = PAST_LEN + PAGE_SIZE


def _nsa_sample_body(pt_ref, q_ref, snew_ref, win_ref, wnew_ref, gate_ref, exp_ref,
                     perm_ref, pe_ref, w1_ref, w2_ref, peo_ref, cache_ref,
                     o_ref, pages_ref, sem_ref, g_ref, skv_ref, wkv_ref, m_ref, l_ref, acc_ref, *, layer):
    slot = _paged_prefetch(pt_ref, cache_ref, pages_ref, sem_ref, layer)
    half = 2 * NSA_HD
    for pq in range(N_PAGES // 4):
        xt = jnp.concatenate(
            [jnp.concatenate([pages_ref[slot, 4 * pq + 2 * a, 0:half, :],
                              pages_ref[slot, 4 * pq + 2 * a + 1, 0:half, :]], axis=1) for a in range(2)],
            axis=0).astype(BF16)
        gall = _dot_nt(perm_ref[...], xt)
        _scatter_pair(gall[:, 0:half], g_ref, 2 * pq)
        _scatter_pair(gall[:, half:2 * half], g_ref, 2 * pq + 1)
    ckv = _compress_finish(g_ref, pe_ref, w1_ref, w2_ref, peo_ref, nb=PAST_LEN // CMP_BLOCK)
    for i in range(N_PAGES):
        skv_ref[:, i * PAGE_SIZE:(i + 1) * PAGE_SIZE] = pages_ref[slot, i, half:2 * half, :].astype(BF16)
    skv_ref[:, PAST_LEN:_SAMPLE_LK] = snew_ref[0]
    wkv_ref[:, 0:WINDOW] = win_ref[...].astype(BF16)
    wkv_ref[:, WINDOW:WINDOW + PAGE_SIZE] = wnew_ref[0]
    tpos = PAST_LEN + lax.broadcasted_iota(jnp.int32, (TS, 1), 0)
    n_sel = -(-(PAST_LEN + DEC_SEQ) // SEL_BLOCK)

    o = _nsa_core(q_ref[0], tpos, ckv, lambda c: skv_ref[...], lambda c: exp_ref[...], 1, _SAMPLE_LK,
                  wkv_ref[...], PAST_LEN - WINDOW, gate_ref[0],
                  m_ref, l_ref, acc_ref, tq=TS, n_cmp=PAST_LEN // CMP_BLOCK, n_sel=n_sel, nsl=2 * LANE,
                  keys_on_lanes=True, nsplit=1)
    o_ref[0] = o
    _paged_drain(cache_ref, pages_ref, sem_ref, layer, slot)


def _sel_expand(nsl, lk):
    return (jnp.arange(nsl)[:, None] == (jnp.arange(lk) // SEL_BLOCK)[None, :]).astype(BF16)


def _nsa_sample(page_flat, qs, snew_t, win_state_t, wnew_t, gates, expand, perm, pe, w1, w2, peo,
                cache_nsa_t, layer):
    r = NSA_HEADS * TS
    nb = PAST_LEN // CMP_BLOCK
    b3 = lambda b, pt: (b, 0, 0)
    c2 = lambda b, pt: (0, 0)
    grid_spec = pltpu.PrefetchScalarGridSpec(
        num_scalar_prefetch=1,
        grid=(DEC_BATCH,),
        in_specs=[
            pl.BlockSpec((1, r, LANE), b3),
            pl.BlockSpec((1, LANE, PAGE_SIZE), b3),
            pl.BlockSpec((None, None, LANE, WINDOW), lambda b, pt: (layer, b, 0, 0)),
            pl.BlockSpec((1, LANE, PAGE_SIZE), b3),
            pl.BlockSpec((1, TS, LANE), b3),
            pl.BlockSpec((2 * LANE, _SAMPLE_LK), c2),
            pl.BlockSpec((_PAIR, _PAIR), c2),
            pl.BlockSpec((CMP_BLOCK // 2, 2 * LANE), c2),
            pl.BlockSpec((CMP_BLOCK // 2, 2 * LANE, 2 * CMP_HIDDEN), lambda b, pt: (0, 0, 0)),
            pl.BlockSpec((2 * CMP_HIDDEN, LANE), c2),
            pl.BlockSpec((_NCL, nb), c2),
            pl.BlockSpec(memory_space=pl.ANY),
        ],
        out_specs=pl.BlockSpec((1, r, LANE), b3),
        scratch_shapes=[
            pltpu.VMEM((2, N_PAGES, 4 * NSA_HD, PAGE_SIZE), F32),
            pltpu.SemaphoreType.DMA((2,)),
            pltpu.VMEM((CMP_BLOCK // 2, nb + 8, 2 * LANE), F32),
            pltpu.VMEM((LANE, _SAMPLE_LK), BF16),
            pltpu.VMEM((LANE, WINDOW + PAGE_SIZE), BF16),
            pltpu.VMEM((r, LANE), F32), pltpu.VMEM((r, LANE), F32), pltpu.VMEM((r, LANE), F32),
        ],
    )
    return pl.pallas_call(
        functools.partial(_nsa_sample_body, layer=layer),
        grid_spec=grid_spec,
        out_shape=jax.ShapeDtypeStruct((DEC_BATCH, r, LANE), F32),
        compiler_params=_params(("arbitrary",)),
        name="nsa_sample",
    )(page_flat, qs, snew_t, win_state_t, wnew_t, gates, expand, perm, pe, w1, w2, peo, cache_nsa_t)


_CONV_PAD = 32
CONV_PAD_W = 32


def _conv_rows(xbuf, w, bias, g, beta, t_out):
    acc = jnp.zeros((t_out, CONV_CH), F32) + bias
    off = _CONV_PAD - (CONV_W - 1)
    for k in range(CONV_W):
        acc = acc + xbuf[off + k:off + k + t_out, :] * w[k:k + 1, :]
    mu = jnp.mean(acc, axis=-1, keepdims=True)
    var = jnp.mean(jnp.square(acc - mu), axis=-1, keepdims=True)
    y = (acc - mu) * lax.rsqrt(var + EPS) * g + beta
    return y * jax.nn.sigmoid(y)


def _conv_prompt_body(main_ref, halo_ref, w_ref, b_ref, g_ref, beta_ref, o_ref, *, tt):
    xbuf = jnp.concatenate([main_ref[0], halo_ref[0]], axis=0)
    o_ref[0] = _conv_rows(xbuf, w_ref[...], b_ref[...], g_ref[...], beta_ref[...], tt)


def _conv_prompt(buf, w, b, g, beta, *, tt=256):
    nt = SEQ // tt
    vec = lambda bb, j: (0, 0)
    return pl.pallas_call(
        functools.partial(_conv_prompt_body, tt=tt),
        grid=(BATCH, nt),
        in_specs=[
            pl.BlockSpec((1, tt, CONV_CH), lambda bb, j: (bb, j, 0)),
            pl.BlockSpec((1, _CONV_PAD, CONV_CH), lambda bb, j: (bb, (j + 1) * (tt // _CONV_PAD), 0)),
            pl.BlockSpec((CONV_PAD_W, CONV_CH), vec),
            pl.BlockSpec((1, CONV_CH), vec),
            pl.BlockSpec((1, CONV_CH), vec),
            pl.BlockSpec((1, CONV_CH), vec),
        ],
        out_specs=pl.BlockSpec((1, tt, CONV_CH), lambda bb, j: (bb, j, 0)),
        out_shape=jax.ShapeDtypeStruct((BATCH, SEQ, CONV_CH), F32),
        compiler_params=_params(("arbitrary", "arbitrary")),
        name="conv_prompt",
    )(buf, buf, w, b, g, beta)


def _conv_sample_body(buf_ref, w_ref, b_ref, g_ref, beta_ref, o_ref, *, nb):
    for i in range(nb):
        o_ref[i] = _conv_rows(buf_ref[i], w_ref[...], b_ref[...], g_ref[...], beta_ref[...], TS)


def _conv_sample(buf, w, b, g, beta, *, nb=16):
    rows = buf.shape[1]
    vec = lambda i: (0, 0)
    return pl.pallas_call(
        functools.partial(_conv_sample_body, nb=nb),
        grid=(DEC_BATCH // nb,),
        in_specs=[
            pl.BlockSpec((nb, rows, CONV_CH), lambda i: (i, 0, 0)),
            pl.BlockSpec((CONV_PAD_W, CONV_CH), vec),
            pl.BlockSpec((1, CONV_CH), vec),
            pl.BlockSpec((1, CONV_CH), vec),
            pl.BlockSpec((1, CONV_CH), vec),
        ],
        out_specs=pl.BlockSpec((nb, TS, CONV_CH), lambda i: (i, 0, 0)),
        out_shape=jax.ShapeDtypeStruct((DEC_BATCH, TS, CONV_CH), F32),
        compiler_params=_params(("arbitrary",)),
        name="conv_sample",
    )(buf, w, b, g, beta)


def _merge_body(x_ref, ol_ref, on_ref, oc_ref, wuv_ref, gn_ref, wo_ref, o_ref):
    om = jnp.concatenate(
        [_dot(ol_ref[:, h * MLA_KV_RANK:(h + 1) * MLA_KV_RANK], wuv_ref[h]) for h in range(MLA_HEADS)], axis=1)
    a1 = MLA_WIDTH
    a2 = MLA_WIDTH + NSA_WIDTH
    parts = jnp.concatenate([
        _rms(om, gn_ref[:, 0:a1]).astype(BF16),
        _rms(on_ref[...], gn_ref[:, a1:a2]).astype(BF16),
        _rms(oc_ref[...], gn_ref[:, a2:D_MODEL]).astype(BF16),
    ], axis=1)
    o_ref[...] = x_ref[...] + _dot(parts, wo_ref[...])


def _merge(x, o_lat, o_nsa, o_conv, wuv, gn, wo, *, tm=512):
    m = x.shape[0]
    row = lambda i: (i, 0)
    return pl.pallas_call(
        _merge_body,
        grid=(m // tm,),
        in_specs=[
            pl.BlockSpec((tm, D_MODEL), row),
            pl.BlockSpec((tm, MLA_HEADS * MLA_KV_RANK), row),
            pl.BlockSpec((tm, NSA_WIDTH), row),
            pl.BlockSpec((tm, CONV_CH), row),
            pl.BlockSpec((MLA_HEADS, MLA_KV_RANK, MLA_V), lambda i: (0, 0, 0)),
            pl.BlockSpec((1, D_MODEL), lambda i: (0, 0)),
            pl.BlockSpec((D_MODEL, D_MODEL), lambda i: (0, 0)),
        ],
        out_specs=pl.BlockSpec((tm, D_MODEL), row),
        out_shape=jax.ShapeDtypeStruct((m, D_MODEL), F32),
        compiler_params=_params(("arbitrary",)),
        name="merge",
    )(x, o_lat, o_nsa, o_conv, wuv, gn, wo)


def _mlp_body(x_ref, g_ref, w1_ref, w2_ref, o_ref, xn_ref, acc_ref):
    j = pl.program_id(1)

    @pl.when(j == 0)
    def _():
        xn_ref[...] = _rms(x_ref[...], g_ref[...]).astype(BF16)
        acc_ref[...] = jnp.zeros_like(acc_ref)

    h = jnp.maximum(_dot(xn_ref[...], w1_ref[...]), 0.0)
    acc_ref[...] += _dot((h * h).astype(BF16), w2_ref[...])

    @pl.when(j == pl.num_programs(1) - 1)
    def _():
        o_ref[...] = x_ref[...] + acc_ref[...]


def _mlp(x, g, w1, w2, *, tm=512, tf=512):
    m = x.shape[0]
    return pl.pallas_call(
        _mlp_body,
        grid=(m // tm, D_FF // tf),
        in_specs=[
            pl.BlockSpec((tm, D_MODEL), lambda i, j: (i, 0)),
            pl.BlockSpec((1, D_MODEL), lambda i, j: (0, 0)),
            pl.BlockSpec((D_MODEL, tf), lambda i, j: (0, j)),
            pl.BlockSpec((tf, D_MODEL), lambda i, j: (j, 0)),
        ],
        out_specs=pl.BlockSpec((tm, D_MODEL), lambda i, j: (i, 0)),
        out_shape=jax.ShapeDtypeStruct((m, D_MODEL), F32),
        scratch_shapes=[pltpu.VMEM((tm, D_MODEL), BF16), pltpu.VMEM((tm, D_MODEL), F32)],
        compiler_params=_params(("arbitrary", "arbitrary")),
        name="mlp",
    )(x, g, w1, w2)


def _final_norm_body(x_ref, g_ref, o_ref):
    o_ref[...] = _rms(x_ref[...], g_ref[...])


def _final_norm(x, g, *, tm=512):
    m = x.shape[0]
    return pl.pallas_call(
        _final_norm_body,
        grid=(m // tm,),
        in_specs=[pl.BlockSpec((tm, D_MODEL), lambda i: (i, 0)), pl.BlockSpec((1, D_MODEL), lambda i: (0, 0))],
        out_specs=pl.BlockSpec((tm, D_MODEL), lambda i: (i, 0)),
        out_shape=jax.ShapeDtypeStruct((m, D_MODEL), F32),
        compiler_params=_params(("arbitrary",)),
        name="final_norm",
    )(x, g)


def _pad_rows(w, n):
    return jnp.pad(w, ((0, n - w.shape[0]), (0, 0)))


def _pack_w_in(w_t):
    segs = [
        w_t[OFF_CQ:OFF_CKV],
        w_t[OFF_CKV:OFF_KR],
        _pad_rows(w_t[OFF_KR:OFF_NQ], LANE),
        w_t[OFF_NQ:OFF_NKV],
        w_t[OFF_NKV:OFF_GATE],
        _pad_rows(w_t[OFF_GATE:OFF_CONV], LANE),
        w_t[OFF_CONV:IN_COLS],
    ]
    return jnp.concatenate(segs, axis=0).astype(BF16)


def _pack_w_q_up(w):
    w = w.reshape(MLA_Q_RANK, MLA_HEADS, MLA_NOPE + MLA_ROPE)
    nope = w[:, :, :MLA_NOPE].reshape(MLA_Q_RANK, MLA_HEADS * MLA_NOPE)
    rope = w[:, :, MLA_NOPE:].reshape(MLA_Q_RANK, MLA_HEADS * MLA_ROPE)
    return jnp.concatenate([nope, rope], axis=1).astype(BF16)


def _pack_cmp(pe, w1, w2):
    pe_t = jnp.concatenate([pe[0], pe[1]], axis=1)
    w1k = w1[0].reshape(CMP_BLOCK, NSA_HD, CMP_HIDDEN)
    w1v = w1[1].reshape(CMP_BLOCK, NSA_HD, CMP_HIDDEN)
    zero = jnp.zeros_like(w1k)
    w1b = jnp.concatenate([jnp.concatenate([w1k, zero], axis=2),
                           jnp.concatenate([zero, w1v], axis=2)], axis=1)
    z2 = jnp.zeros_like(w2[0])
    w2b = jnp.concatenate([jnp.concatenate([w2[0], z2], axis=1),
                           jnp.concatenate([z2, w2[1]], axis=1)], axis=0)
    pe2 = pe_t.reshape(CMP_BLOCK // 2, 2 * LANE)
    w1p = w1b.reshape(CMP_BLOCK // 2, 2 * LANE, 2 * CMP_HIDDEN)
    return pe2, w1p.astype(BF16), w2b.astype(BF16)


def _rope_tables():
    half = NSA_HD // 2
    freq = ROPE_THETA ** (-jnp.arange(half, dtype=F32) / half)
    pos = jnp.concatenate([jnp.arange(SEQ, dtype=jnp.int32),
                           PAST_LEN + (jnp.arange(256, dtype=jnp.int32) % DEC_SEQ)])
    ang = pos.astype(F32)[:, None] * freq[None, :]
    cos = jnp.tile(jnp.cos(ang), (1, LANE // half))
    sin = jnp.tile(jnp.sin(ang), (1, LANE // half))
    return cos, sin


def kernel(x_prompt, x_sample, cache_mla, cache_nsa, state_nsa_win, state_conv, page_table, norm_mix, w_in, mla_q_norm, mla_kv_norm, mla_w_q_up, mla_w_uk, mla_w_uv, nsa_cmp_pe, nsa_cmp_w1, nsa_cmp_w2, conv_w, conv_b, conv_ln_g, conv_ln_b, group_norm, w_out, norm_mlp, mlp_w1, mlp_w2, norm_final):
    x = jnp.concatenate([x_prompt.reshape(MP, D_MODEL), x_sample.reshape(MS, D_MODEL)], axis=0)
    cos_t, sin_t = _rope_tables()
    page_flat = page_table.reshape(-1).astype(jnp.int32)
    n_pool = cache_nsa.shape[1]
    cache_mla_t = jnp.swapaxes(cache_mla, 2, 3)
    cache_nsa_t = jnp.transpose(cache_nsa, (0, 1, 3, 4, 5, 2)).reshape(DEPTH, n_pool, 4 * NSA_HD, PAGE_SIZE)
    win_t = jnp.transpose(state_nsa_win, (0, 1, 3, 4, 5, 2)).reshape(DEPTH, DEC_BATCH, 2 * NSA_HD, WINDOW)
    perm, peo_p = _perm_matrices(SEQ // CMP_BLOCK)
    _, peo_s = _perm_matrices(PAST_LEN // CMP_BLOCK)
    sel_expand = _sel_expand(2 * LANE, _SAMPLE_LK)
    w_in_t = jnp.transpose(w_in, (2, 0, 1))

    mla_p, mla_s, nsa_p, nsa_s, win_p, win_s, conv_p, conv_s = [], [], [], [], [], [], [], []
    for l in range(DEPTH):
        wp = _pack_w_in(w_in_t[:, l, :])
        wq = _pack_w_q_up(mla_w_q_up[l])
        wuk = jnp.transpose(mla_w_uk[l], (1, 2, 0)).astype(BF16)
        wuv = jnp.transpose(mla_w_uv[l], (1, 0, 2)).astype(BF16)
        pe_t, w1b, w2b = _pack_cmp(nsa_cmp_pe[l], nsa_cmp_w1[l], nsa_cmp_w2[l])
        cw = jnp.pad(conv_w[l], ((0, CONV_PAD_W - CONV_W), (0, 0)))

        z = _inproj(x, norm_mix[l][None], wp)
        q384, kvb, mla_row, nq, rows, rowsb, gate, u = _post(
            z, cos_t, sin_t, mla_q_norm[l][None], mla_kv_norm[l][None], wq, wuk)

        mla_p.append(mla_row[:MP, :MLA_ROW].reshape(BATCH, SEQ, MLA_ROW))
        mla_s.append(mla_row[MP:, :MLA_ROW].reshape(DEC_BATCH, DEC_SEQ, MLA_ROW))
        nsa_p.append(rows[:MP, :4 * NSA_HD].reshape(BATCH, SEQ, 4, 1, NSA_HD))
        nsa_s.append(rows[MP:, :4 * NSA_HD].reshape(DEC_BATCH, DEC_SEQ, 4, 1, NSA_HD))
        win_rows_p = rows[:MP, 4 * NSA_HD:].reshape(BATCH, SEQ, 2, 1, NSA_HD)
        win_p.append(win_rows_p[:, SEQ - WINDOW:])
        win_rows_s = rows[MP:, 4 * NSA_HD:].reshape(DEC_BATCH, DEC_SEQ, 2, 1, NSA_HD)
        win_s.append(jnp.concatenate([state_nsa_win[l][:, DEC_SEQ:], win_rows_s], axis=1))
        up = u[:MP].reshape(BATCH, SEQ, CONV_CH)
        us = u[MP:].reshape(DEC_BATCH, DEC_SEQ, CONV_CH)
        bufs = jnp.concatenate([state_conv[l], us], axis=1)
        conv_p.append(up[:, SEQ - (CONV_W - 1):])
        conv_s.append(bufs[:, DEC_SEQ:])

        ol_p = _mla_prompt(q384, kvb)
        qs = q384[:, MP:].reshape(MLA_HEADS, DEC_BATCH, DEC_SEQ, KV_PAD)
        qs = jnp.transpose(qs, (1, 0, 2, 3)).reshape(DEC_BATCH, MLA_HEADS * DEC_SEQ, KV_PAD)
        knew = jnp.pad(kvb[MP:].reshape(DEC_BATCH, DEC_SEQ, KV_PAD), ((0, 0), (0, PAGE_SIZE - DEC_SEQ), (0, 0)))
        ol_s = _mla_sample(page_flat, qs, jnp.swapaxes(knew, 1, 2), cache_mla_t, l)
        ol_s = ol_s.reshape(DEC_BATCH, MLA_HEADS, DEC_SEQ, MLA_KV_RANK)
        ol_s = jnp.transpose(ol_s, (0, 2, 1, 3)).reshape(MS, MLA_HEADS * MLA_KV_RANK)
        o_lat = jnp.concatenate([ol_p, ol_s], axis=0)

        ckv_p = _compress_prompt(rows, perm, pe_t, w1b, w2b, peo_p)
        wpad = jnp.pad(rowsb[:MP, 2 * LANE:].reshape(BATCH, SEQ, LANE), ((0, 0), (WINDOW, 0), (0, 0)))
        on_p = _nsa_prompt(nq, ckv_p, rowsb, wpad, gate)
        on_p = jnp.transpose(on_p[:, :, NSA_HD:], (1, 0, 2)).reshape(MP, NSA_WIDTH)

        pad_tok = ((0, 0), (0, 0), (0, TS - DEC_SEQ), (0, 0))
        nqs = jnp.pad(nq[:, MP:].reshape(NSA_HEADS, DEC_BATCH, DEC_SEQ, LANE), pad_tok)
        nqs = jnp.transpose(nqs, (1, 0, 2, 3)).reshape(DEC_BATCH, NSA_HEADS * TS, LANE)
        pad_rows = ((0, 0), (0, PAGE_SIZE - DEC_SEQ), (0, 0))
        snew = jnp.pad(rowsb[MP:, LANE:2 * LANE].reshape(DEC_BATCH, DEC_SEQ, LANE), pad_rows)
        wnew = jnp.pad(rowsb[MP:, 2 * LANE:].reshape(DEC_BATCH, DEC_SEQ, LANE), pad_rows)
        gs = jnp.pad(gate[MP:].reshape(DEC_BATCH, DEC_SEQ, LANE), ((0, 0), (0, TS - DEC_SEQ), (0, 0)))
        on_s = _nsa_sample(page_flat, nqs, jnp.swapaxes(snew, 1, 2), win_t, jnp.swapaxes(wnew, 1, 2),
                           gs, sel_expand, perm, pe_t, w1b, w2b, peo_s, cache_nsa_t, l)
        on_s = on_s[:, :, NSA_HD:].reshape(DEC_BATCH, NSA_HEADS, TS, NSA_HD)[:, :, :DEC_SEQ]
        on_s = jnp.transpose(on_s, (0, 2, 1, 3)).reshape(MS, NSA_WIDTH)
        o_nsa = jnp.concatenate([on_p, on_s], axis=0)

        bufp = jnp.pad(up, ((0, 0), (_CONV_PAD, 0), (0, 0)))
        oc_p = _conv_prompt(bufp, cw, conv_b[l][None], conv_ln_g[l][None], conv_ln_b[l][None])
        bufs_p = jnp.pad(bufs, ((0, 0), (_CONV_PAD - (CONV_W - 1), TS - DEC_SEQ), (0, 0)))
        oc_s = _conv_sample(bufs_p, cw, conv_b[l][None], conv_ln_g[l][None], conv_ln_b[l][None])
        o_conv = jnp.concatenate([oc_p.reshape(MP, CONV_CH), oc_s[:, :DEC_SEQ].reshape(MS, CONV_CH)], axis=0)

        x = _merge(x, o_lat, o_nsa, o_conv, wuv, group_norm[l][None], w_out[l].astype(BF16))
        x = _mlp(x, norm_mlp[l][None], mlp_w1[l].astype(BF16), mlp_w2[l].astype(BF16))

    y = _final_norm(x, norm_final[None])
    return (y[:MP].reshape(BATCH, SEQ, D_MODEL), y[MP:].reshape(DEC_BATCH, DEC_SEQ, D_MODEL),
            jnp.stack(mla_p), jnp.stack(mla_s), jnp.stack(nsa_p), jnp.stack(nsa_s),
            jnp.stack(win_p), jnp.stack(win_s), jnp.stack(conv_p), jnp.stack(conv_s))
```

```python
import functools
import math

import jax
import jax.numpy as jnp
from jax import lax
from jax.experimental import pallas as pl
from jax.experimental.pallas import tpu as pltpu

F32 = jnp.float32
BF16 = jnp.bfloat16

D_MODEL = 2048
BATCH = 4
SEQ = 2048
DEPTH = 4
DEC_BATCH = 128
DEC_SEQ = 4
PAST_LEN = 8192
PAGE_SIZE = 128
N_PAGES = PAST_LEN // PAGE_SIZE
MLA_NOPE = 128
MLA_ROPE = 64
MLA_V = 128
MLA_HEADS = 6
MLA_Q_RANK = 512
MLA_KV_RANK = 256
MLA_ROW = MLA_KV_RANK + MLA_ROPE
MLA_WIDTH = MLA_HEADS * MLA_V
MLA_SCALE = 1.0 / math.sqrt(MLA_NOPE + MLA_ROPE)
NSA_HD = 64
NSA_HEADS = 12
NSA_WIDTH = NSA_HEADS * NSA_HD
NSA_SCALE = 1.0 / math.sqrt(NSA_HD)
CMP_BLOCK = 32
CMP_HIDDEN = 128
SEL_BLOCK = 64
SEL_TOPK = 16
WINDOW = 512
SEL_FORCE = 1e4
CONV_CH = 512
CONV_W = 31
D_FF = 4 * D_MODEL
ROPE_THETA = 10000.0
EPS = 1e-6

OFF_CQ = 0
OFF_CKV = OFF_CQ + MLA_Q_RANK
OFF_KR = OFF_CKV + MLA_KV_RANK
OFF_NQ = OFF_KR + MLA_ROPE
OFF_NKV = OFF_NQ + NSA_WIDTH
OFF_GATE = OFF_NKV + 6 * NSA_HD
OFF_CONV = OFF_GATE + 3 * NSA_HEADS
IN_COLS = OFF_CONV + 2 * CONV_CH

LANE = 128
VMEM_LIMIT = 56 * 1024 * 1024

MP = BATCH * SEQ
MS = DEC_BATCH * DEC_SEQ
MT = MP + MS
TS = 8
KV_PAD = 3 * LANE

Z_CQ = 0
Z_CKV = 512
Z_KR = 768
Z_NQ = 896
Z_NKV = 1664
Z_GATE = 2048
Z_CA = 2176
Z_CB = 2688
NZ = 3200

NEG = -1e30


def _dot(a, b):
    return jnp.dot(a, b, preferred_element_type=F32)


def _dot_nt(a, b):
    return lax.dot_general(a, b, (((1,), (1,)), ((), ())), preferred_element_type=F32)


def _rms(x, g):
    return x * lax.rsqrt(jnp.mean(x * x, axis=-1, keepdims=True) + EPS) * g


def _rep(x, n):
    return x if n == 1 else jnp.concatenate([x] * n, axis=1)


def _params(sem):
    return pltpu.CompilerParams(dimension_semantics=sem, vmem_limit_bytes=VMEM_LIMIT)


def _inproj_body(x_ref, g_ref, w_ref, o_ref, xn_ref):
    @pl.when(pl.program_id(1) == 0)
    def _():
        xn_ref[...] = _rms(x_ref[...], g_ref[...]).astype(BF16)

    o_ref[...] = _dot_nt(xn_ref[...], w_ref[...])


def _inproj(x, g, w_t, *, tm=512, tn=3200):
    m = x.shape[0]
    return pl.pallas_call(
        _inproj_body,
        grid=(m // tm, NZ // tn),
        in_specs=[
            pl.BlockSpec((tm, D_MODEL), lambda i, j: (i, 0)),
            pl.BlockSpec((1, D_MODEL), lambda i, j: (0, 0)),
            pl.BlockSpec((tn, D_MODEL), lambda i, j: (j, 0)),
        ],
        out_specs=pl.BlockSpec((tm, tn), lambda i, j: (i, j)),
        out_shape=jax.ShapeDtypeStruct((m, NZ), F32),
        scratch_shapes=[pltpu.VMEM((tm, D_MODEL), BF16)],
        compiler_params=_params(("arbitrary", "arbitrary")),
        name="inproj",
    )(x, g, w_t)


def _post_body(z_ref, cos_ref, sin_ref, qn_ref, kvn_ref, wq_ref, wuk_ref,
               q_ref, kvb_ref, mla_ref, nq_ref, rows_ref, rowsb_ref, gate_ref, u_ref, *, tm):
    cos = cos_ref[...]
    sin = sin_ref[...]
    lane = lax.broadcasted_iota(jnp.int32, (tm, LANE), 1)
    first = (lane & 63) < 32
    lo = lane < 64
    sa = jnp.where(first, -sin, 0.0)
    sb = jnp.where(first, 0.0, sin)
    ch = jnp.where(lo, cos, 1.0)
    sah = jnp.where(lo, sa, 0.0)
    sbh = jnp.where(lo, sb, 0.0)

    def rope_full(x):
        return x * cos + pltpu.roll(x, 96, 1) * sa + pltpu.roll(x, 32, 1) * sb

    def rope_half(x):
        return x * ch + pltpu.roll(x, 96, 1) * sah + pltpu.roll(x, 32, 1) * sbh

    def split_heads(t):
        return jnp.where(lo, t, 0.0), jnp.where(lo, pltpu.roll(t, 64, 1), 0.0)

    cq = _rms(z_ref[:, Z_CQ:Z_CQ + MLA_Q_RANK], qn_ref[...]).astype(BF16)
    q = _dot(cq, wq_ref[...])
    for h in range(MLA_HEADS):
        ql = _dot(q[:, h * MLA_NOPE:(h + 1) * MLA_NOPE].astype(BF16), wuk_ref[h])
        q_ref[h, :, 0:MLA_KV_RANK] = (ql * MLA_SCALE).astype(BF16)
    rope0 = MLA_HEADS * MLA_NOPE
    for j in range(MLA_HEADS // 2):
        t = rope_full(q[:, rope0 + j * LANE:rope0 + (j + 1) * LANE]) * MLA_SCALE
        a, b = split_heads(t)
        q_ref[2 * j, :, MLA_KV_RANK:KV_PAD] = a.astype(BF16)
        q_ref[2 * j + 1, :, MLA_KV_RANK:KV_PAD] = b.astype(BF16)

    ckv = _rms(z_ref[:, Z_CKV:Z_CKV + MLA_KV_RANK], kvn_ref[...])
    kr = rope_half(z_ref[:, Z_KR:Z_KR + LANE])
    mla_ref[:, 0:MLA_KV_RANK] = ckv
    mla_ref[:, MLA_KV_RANK:KV_PAD] = kr
    kvb_ref[:, 0:MLA_KV_RANK] = ckv.astype(BF16)
    kvb_ref[:, MLA_KV_RANK:KV_PAD] = kr.astype(BF16)

    for j in range(NSA_HEADS // 2):
        t = rope_full(z_ref[:, Z_NQ + j * LANE:Z_NQ + (j + 1) * LANE]) * NSA_SCALE
        a, b = split_heads(t)
        nq_ref[2 * j] = a.astype(BF16)
        nq_ref[2 * j + 1] = b.astype(BF16)

    for j in range(3):
        t = rope_half(z_ref[:, Z_NKV + j * LANE:Z_NKV + (j + 1) * LANE])
        rows_ref[:, j * LANE:(j + 1) * LANE] = t
        rowsb_ref[:, j * LANE:(j + 1) * LANE] = t.astype(BF16)

    gate_ref[...] = jax.nn.sigmoid(z_ref[:, Z_GATE:Z_GATE + LANE])
    u_ref[...] = z_ref[:, Z_CA:Z_CA + CONV_CH] * jax.nn.sigmoid(z_ref[:, Z_CB:Z_CB + CONV_CH])


def _post(z, cos_t, sin_t, qn, kvn, wq, wuk, *, tm=256):
    m = z.shape[0]
    n_prompt_tiles = MP // tm
    tiles_per_seq = SEQ // tm

    def tab_map(i):
        return (jnp.where(i < n_prompt_tiles, i % tiles_per_seq, tiles_per_seq), 0)

    row = lambda i: (i, 0)
    const2 = lambda i: (0, 0)
    return pl.pallas_call(
        functools.partial(_post_body, tm=tm),
        grid=(m // tm,),
        in_specs=[
            pl.BlockSpec((tm, NZ), row),
            pl.BlockSpec((tm, LANE), tab_map),
            pl.BlockSpec((tm, LANE), tab_map),
            pl.BlockSpec((1, MLA_Q_RANK), const2),
            pl.BlockSpec((1, MLA_KV_RANK), const2),
            pl.BlockSpec((MLA_Q_RANK, MLA_HEADS * (MLA_NOPE + MLA_ROPE)), const2),
            pl.BlockSpec((MLA_HEADS, MLA_NOPE, MLA_KV_RANK), lambda i: (0, 0, 0)),
        ],
        out_specs=[
            pl.BlockSpec((MLA_HEADS, tm, KV_PAD), lambda i: (0, i, 0)),
            pl.BlockSpec((tm, KV_PAD), row),
            pl.BlockSpec((tm, KV_PAD), row),
            pl.BlockSpec((NSA_HEADS, tm, LANE), lambda i: (0, i, 0)),
            pl.BlockSpec((tm, 3 * LANE), row),
            pl.BlockSpec((tm, 3 * LANE), row),
            pl.BlockSpec((tm, LANE), row),
            pl.BlockSpec((tm, CONV_CH), row),
        ],
        out_shape=[
            jax.ShapeDtypeStruct((MLA_HEADS, m, KV_PAD), BF16),
            jax.ShapeDtypeStruct((m, KV_PAD), BF16),
            jax.ShapeDtypeStruct((m, KV_PAD), F32),
            jax.ShapeDtypeStruct((NSA_HEADS, m, LANE), BF16),
            jax.ShapeDtypeStruct((m, 3 * LANE), F32),
            jax.ShapeDtypeStruct((m, 3 * LANE), BF16),
            jax.ShapeDtypeStruct((m, LANE), F32),
            jax.ShapeDtypeStruct((m, CONV_CH), F32),
        ],
        compiler_params=_params(("arbitrary",)),
        name="post_proj",
    )(z, cos_t, sin_t, qn, kvn, wq, wuk)


def _mla_prompt_body(q_ref, kv_ref, o_ref, m_ref, l_ref, acc_ref, *, tq, tk, nsplit):
    qi = pl.program_id(1)
    r = MLA_HEADS * tq
    q = q_ref[...].reshape(r, KV_PAD)
    m_ref[...] = jnp.full((r, LANE), NEG, F32)
    l_ref[...] = jnp.zeros((r, LANE), F32)
    acc_ref[...] = jnp.zeros((r, MLA_KV_RANK), F32)
    rg = r // nsplit
    qpos = qi * tq + (lax.broadcasted_iota(jnp.int32, (rg, tk), 0) & (tq - 1))
    col = lax.broadcasted_iota(jnp.int32, (rg, tk), 1)
    nchunks = (qi * tq + tq + tk - 1) // tk

    def step(j, carry):
        k = kv_ref[pl.ds(pl.multiple_of(j * tk, tk), tk), :]
        for g in range(nsplit):
            rows = slice(g * rg, (g + 1) * rg)
            s = _dot_nt(q[rows], k)
            s = jnp.where(j * tk + col <= qpos, s, NEG)
            m_old = m_ref[rows, :]
            m_new = jnp.maximum(m_old, jnp.max(s, axis=1, keepdims=True))
            alpha = jnp.exp(m_old - m_new)
            p = jnp.exp(s - _rep(m_new, tk // LANE))
            l_ref[rows, :] = alpha * l_ref[rows, :] + jnp.sum(p, axis=1, keepdims=True)
            acc_ref[rows, :] = (_rep(alpha, MLA_KV_RANK // LANE) * acc_ref[rows, :]
                                + _dot(p.astype(BF16), k[:, 0:MLA_KV_RANK]))
            m_ref[rows, :] = m_new
        return carry

    lax.fori_loop(0, nchunks, step, 0)
    o = acc_ref[...] / _rep(l_ref[...], MLA_KV_RANK // LANE)
    for h in range(MLA_HEADS):
        o_ref[:, h * MLA_KV_RANK:(h + 1) * MLA_KV_RANK] = o[h * tq:(h + 1) * tq].astype(BF16)


def _mla_prompt(q, kvb, *, tq=256, tk=512, nsplit=2):
    nq = SEQ // tq
    r = MLA_HEADS * tq
    return pl.pallas_call(
        functools.partial(_mla_prompt_body, tq=tq, tk=tk, nsplit=nsplit),
        grid=(BATCH, nq),
        in_specs=[
            pl.BlockSpec((MLA_HEADS, tq, KV_PAD), lambda b, i: (0, b * nq + i, 0)),
            pl.BlockSpec((SEQ, KV_PAD), lambda b, i: (b, 0)),
        ],
        out_specs=pl.BlockSpec((tq, MLA_HEADS * MLA_KV_RANK), lambda b, i: (b * nq + i, 0)),
        out_shape=jax.ShapeDtypeStruct((MP, MLA_HEADS * MLA_KV_RANK), BF16),
        scratch_shapes=[pltpu.VMEM((r, LANE), F32), pltpu.VMEM((r, LANE), F32), pltpu.VMEM((r, MLA_KV_RANK), F32)],
        compiler_params=_params(("arbitrary", "arbitrary")),
        name="mla_prompt",
    )(q, kvb)


def _page_copy(cache_ref, pages_ref, sem_ref, layer, page, slot, i):
    return pltpu.make_async_copy(cache_ref.at[layer, page], pages_ref.at[slot, i], sem_ref.at[slot])


def _paged_prefetch(pt_ref, cache_ref, pages_ref, sem_ref, layer):
    ns = pages_ref.shape[0]
    dist = ns - 1
    b = pl.program_id(0)
    nb = pl.num_programs(0)
    slot = lax.rem(b, ns)

    @pl.when(b == 0)
    def _():
        for d in range(dist):
            for i in range(N_PAGES):
                _page_copy(cache_ref, pages_ref, sem_ref, layer, pt_ref[d * N_PAGES + i], d, i).start()

    for i in range(N_PAGES):
        _page_copy(cache_ref, pages_ref, sem_ref, layer, 0, slot, i).wait()
    nxt = lax.rem(b + dist, nb)
    nslot = lax.rem(b + dist, ns)
    for i in range(N_PAGES):
        _page_copy(cache_ref, pages_ref, sem_ref, layer, pt_ref[nxt * N_PAGES + i], nslot, i).start()
    return slot


def _paged_drain(cache_ref, pages_ref, sem_ref, layer, slot):
    ns = pages_ref.shape[0]
    b = pl.program_id(0)

    @pl.when(b == pl.num_programs(0) - 1)
    def _():
        for d in range(1, ns):
            for i in range(N_PAGES):
                _page_copy(cache_ref, pages_ref, sem_ref, layer, 0, lax.rem(b + d, ns), i).wait()


def _mla_sample_body(pt_ref, q_ref, knew_ref, cache_ref, o_ref, pages_ref, sem_ref, kt_ref, *, layer):
    slot = _paged_prefetch(pt_ref, cache_ref, pages_ref, sem_ref, layer)
    lk = PAST_LEN + PAGE_SIZE
    kt_ref[MLA_ROW:KV_PAD, :] = jnp.zeros((KV_PAD - MLA_ROW, lk), BF16)
    for i in range(N_PAGES):
        kt_ref[0:MLA_ROW, i * PAGE_SIZE:(i + 1) * PAGE_SIZE] = pages_ref[slot, i].astype(BF16)
    kt_ref[:, PAST_LEN:lk] = knew_ref[0]
    q = q_ref[0]
    r = q.shape[0]
    s = _dot(q, kt_ref[...])
    col = lax.broadcasted_iota(jnp.int32, (r, lk), 1)
    row = lax.broadcasted_iota(jnp.int32, (r, lk), 0)
    valid = (col < PAST_LEN) | ((col - PAST_LEN) <= (row & (DEC_SEQ - 1)))
    s = jnp.where(valid, s, NEG)
    m = jnp.max(s, axis=1, keepdims=True)
    p = jnp.exp(s - m)
    l = jnp.sum(p, axis=1, keepdims=True)
    o = _dot_nt(p.astype(BF16), kt_ref[0:MLA_KV_RANK, :]) / l
    o_ref[0] = o.astype(BF16)
    _paged_drain(cache_ref, pages_ref, sem_ref, layer, slot)


_MLA_SLOTS = 3


def _mla_sample(page_flat, qs, knew_t, cache_mla_t, layer):
    r = MLA_HEADS * DEC_SEQ
    grid_spec = pltpu.PrefetchScalarGridSpec(
        num_scalar_prefetch=1,
        grid=(DEC_BATCH,),
        in_specs=[
            pl.BlockSpec((1, r, KV_PAD), lambda b, pt: (b, 0, 0)),
            pl.BlockSpec((1, KV_PAD, PAGE_SIZE), lambda b, pt: (b, 0, 0)),
            pl.BlockSpec(memory_space=pl.ANY),
        ],
        out_specs=pl.BlockSpec((1, r, MLA_KV_RANK), lambda b, pt: (b, 0, 0)),
        scratch_shapes=[
            pltpu.VMEM((_MLA_SLOTS, N_PAGES, MLA_ROW, PAGE_SIZE), F32),
            pltpu.SemaphoreType.DMA((_MLA_SLOTS,)),
            pltpu.VMEM((KV_PAD, PAST_LEN + PAGE_SIZE), BF16),
        ],
    )
    return pl.pallas_call(
        functools.partial(_mla_sample_body, layer=layer),
        grid_spec=grid_spec,
        out_shape=jax.ShapeDtypeStruct((DEC_BATCH, r, MLA_KV_RANK), BF16),
        compiler_params=_params(("arbitrary",)),
        name="mla_sample",
    )(page_flat, qs, knew_t, cache_mla_t)


def _gelu_tanh(x):
    return 0.5 * x * (1.0 + jnp.tanh(math.sqrt(2.0 / math.pi) * (x + 0.044715 * (x * x * x))))


_PAIR = 2 * PAGE_SIZE
_BPP = _PAIR // CMP_BLOCK
_NCL = 2 * LANE


def _scatter_pair(gall, g_ref, pp):
    for r in range(CMP_BLOCK):
        g_ref[r // 2, pp * _BPP:(pp + 1) * _BPP, (r % 2) * LANE:(r % 2 + 1) * LANE] = gall[r * _BPP:(r + 1) * _BPP, :]


def _compress_finish(g_ref, pe_ref, w1_ref, w2_ref, peo_ref, *, nb):
    nslab = CMP_BLOCK // 2
    for q in range(nslab):
        g_ref[q, nb:nb + 8, :] = jnp.broadcast_to(pe_ref[q:q + 1, :], (8, 2 * LANE))
    h0 = jnp.zeros((nb + 8, 2 * CMP_HIDDEN), F32)
    h1 = jnp.zeros((nb + 8, 2 * CMP_HIDDEN), F32)
    for q in range(0, nslab, 2):
        h0 = h0 + _dot(g_ref[q].astype(BF16), w1_ref[q])
        h1 = h1 + _dot(g_ref[q + 1].astype(BF16), w1_ref[q + 1])
    h = h0 + h1
    hb = h[0:nb] + h[nb:nb + 1]
    c = _dot(_gelu_tanh(hb).astype(BF16), w2_ref[...]).astype(BF16)
    return _dot(peo_ref[...], c).astype(BF16)


def _compress_prompt_body(src_ref, perm_ref, pe_ref, w1_ref, w2_ref, peo_ref, o_ref, g_ref):
    for pp in range(SEQ // _PAIR):
        x = src_ref[pp * _PAIR:(pp + 1) * _PAIR, :].astype(BF16)
        _scatter_pair(_dot(perm_ref[...], x), g_ref, pp)
    o_ref[0] = _compress_finish(g_ref, pe_ref, w1_ref, w2_ref, peo_ref, nb=SEQ // CMP_BLOCK)


def _compress_prompt(rows, perm, pe, w1, w2, peo):
    nb = SEQ // CMP_BLOCK
    c2 = lambda b: (0, 0)
    return pl.pallas_call(
        _compress_prompt_body,
        grid=(BATCH,),
        in_specs=[
            pl.BlockSpec((SEQ, LANE), lambda b: (b, 0)),
            pl.BlockSpec((_PAIR, _PAIR), c2),
            pl.BlockSpec((CMP_BLOCK // 2, 2 * LANE), c2),
            pl.BlockSpec((CMP_BLOCK // 2, 2 * LANE, 2 * CMP_HIDDEN), lambda b: (0, 0, 0)),
            pl.BlockSpec((2 * CMP_HIDDEN, LANE), c2),
            pl.BlockSpec((_NCL, nb), c2),
        ],
        out_specs=pl.BlockSpec((1, _NCL, LANE), lambda b: (b, 0, 0)),
        out_shape=jax.ShapeDtypeStruct((BATCH, _NCL, LANE), BF16),
        scratch_shapes=[pltpu.VMEM((CMP_BLOCK // 2, nb + 8, 2 * LANE), F32)],
        compiler_params=_params(("arbitrary",)),
        name="compress_prompt",
    )(rows, perm, pe, w1, w2, peo)


def _perm_matrices(nb):
    i = jnp.arange(_PAIR)
    perm = (jnp.arange(_PAIR)[None, :] == (CMP_BLOCK * (i % _BPP) + i // _BPP)[:, None]).astype(BF16)
    j = jnp.arange(_NCL)
    blk = jnp.where(j < _NCL // 2, 2 * j, 2 * (j - _NCL // 2) + 1)
    peo = (jnp.arange(nb)[None, :] == blk[:, None]).astype(BF16)
    return perm, peo


def _topk_mask_lanes(score, n_sel, k):
    tq, nsl = score.shape
    mm = lax.broadcasted_iota(jnp.int32, (tq, nsl), 1)
    rank = jnp.zeros((tq, nsl), F32)
    for j in range(n_sel):
        sc = score[:, j:j + 1]
        ahead = (sc > score) | ((sc == score) & (mm > j))
        rank = rank + jnp.where(ahead, 1.0, 0.0)
    return jnp.where(rank < float(k), 1.0, 0.0)


def _topk_mask_sublanes(score, n_sel, k):
    st = score.T
    nsl, tq = st.shape
    mm = lax.broadcasted_iota(jnp.int32, (nsl, tq), 0)
    rank = jnp.zeros((nsl, tq), F32)
    for j in range(n_sel):
        sc = st[j:j + 1, :]
        ahead = (sc > st) | ((sc == st) & (mm > j))
        rank = rank + jnp.where(ahead, 1.0, 0.0)
    return jnp.where(rank < float(k), 1.0, 0.0).T


def _nsa_core(q, tpos, ckv, get_chunk, get_expand, nchunks, tk, wkv, wpos0, gate,
              m_ref, l_ref, acc_ref, *, tq, n_cmp, n_sel, nsl, keys_on_lanes, nsplit):
    if keys_on_lanes:
        def qk(a, kv):
            return _dot(a[:, 0:NSA_HD], kv[0:NSA_HD])

        def pv(p, kv):
            o = _dot_nt(p, kv[NSA_HD:2 * NSA_HD])
            return jnp.concatenate([jnp.zeros_like(o), o], axis=1)
    else:
        qk = _dot_nt
        pv = _dot
    r = NSA_HEADS * tq
    qpos = jnp.concatenate([tpos] * NSA_HEADS, axis=0)

    ncl = ckv.shape[0]
    half = ncl // 2
    s_c = _dot_nt(q, ckv)
    jc = lax.broadcasted_iota(jnp.int32, (1, ncl), 1)
    blk = jnp.where(jc < half, 2 * jc, 2 * (jc - half) + 1)
    valid_c = (blk < n_cmp) & (blk * CMP_BLOCK + (CMP_BLOCK - 1) <= qpos)
    s_c = jnp.where(valid_c, s_c, NEG)
    m_c = jnp.broadcast_to(jnp.max(s_c, axis=1, keepdims=True), (r, LANE))
    p = jnp.where(valid_c, jnp.exp(s_c - _rep(m_c, ncl // LANE)), 0.0)
    d = jnp.broadcast_to(jnp.sum(p, axis=1, keepdims=True), (r, LANE))
    p_c = p / _rep(jnp.where(d > 0, d, 1.0), ncl // LANE)
    o_c = _dot(p_c.astype(BF16), ckv)

    imp = p_c[0:tq]
    for h in range(1, NSA_HEADS):
        imp = imp + p_c[h * tq:(h + 1) * tq]
    imp_sel = imp[:, 0:half] + imp[:, half:ncl]
    if nsl > half:
        imp_sel = jnp.concatenate([imp_sel, jnp.zeros((tq, nsl - half), F32)], axis=1)
    mm = lax.broadcasted_iota(jnp.int32, (tq, nsl), 1)
    cur = tpos >> 6
    valid_s = (mm * SEL_BLOCK <= tpos) & (mm < n_sel)
    forced = (mm == 0) | (mm == cur) | (mm == cur - 1)
    score = jnp.where(valid_s, jnp.where(forced, SEL_FORCE, imp_sel), -jnp.inf)
    topk = min(SEL_TOPK, n_sel)
    if tq == nsl:
        selb = _topk_mask_sublanes(score, n_sel, topk).astype(BF16)
    else:
        selb = _topk_mask_lanes(score, n_sel, topk).astype(BF16)

    hg = NSA_HEADS // nsplit
    rg = hg * tq
    kcol = lax.broadcasted_iota(jnp.int32, (tq, tk), 1)

    def step(c, first):
        k = get_chunk(c)
        chosen = _dot(selb, get_expand(c)) > 0.5
        mask = (chosen & (kcol + c * tk <= tpos))[None]
        for g in range(nsplit):
            rows = slice(g * rg, (g + 1) * rg)
            s = qk(q[rows], k).reshape(hg, tq, tk)
            s = jnp.where(mask, s, NEG).reshape(rg, tk)
            m_cur = jnp.max(s, axis=1, keepdims=True)
            if first:
                m_new = jnp.broadcast_to(m_cur, (rg, LANE))
            else:
                m_old = m_ref[rows, :]
                m_new = jnp.maximum(m_old, m_cur)
                alpha = jnp.exp(m_old - m_new)
            pp = jnp.exp(s - _rep(m_new, tk // LANE))
            psum = jnp.sum(pp, axis=1, keepdims=True)
            acc = pv(pp.astype(BF16), k)
            if first:
                l_ref[rows, :] = jnp.broadcast_to(psum, (rg, LANE))
                acc_ref[rows, :] = acc
            else:
                l_ref[rows, :] = alpha * l_ref[rows, :] + psum
                acc_ref[rows, :] = alpha * acc_ref[rows, :] + acc
            m_ref[rows, :] = m_new

    step(0, True)
    if not (isinstance(nchunks, int) and nchunks == 1):
        def body(c, carry):
            step(c, False)
            return carry
        lax.fori_loop(1, nchunks, body, 0)
    o_s = acc_ref[...] / l_ref[...]

    lw = wkv.shape[1] if keys_on_lanes else wkv.shape[0]
    wpos = wpos0 + lax.broadcasted_iota(jnp.int32, (1, lw), 1)
    dist = tpos - wpos
    mask_w = (dist >= 0) & (dist < WINDOW) & (wpos >= 0)
    o_groups = []
    for g in range(nsplit):
        s_w = qk(q[g * rg:(g + 1) * rg], wkv).reshape(hg, tq, lw)
        s_w = jnp.where(mask_w[None], s_w, NEG).reshape(rg, lw)
        m_w = jnp.broadcast_to(jnp.max(s_w, axis=1, keepdims=True), (rg, LANE))
        p_w = jnp.exp(s_w - _rep(m_w, lw // LANE))
        l_w = jnp.broadcast_to(jnp.sum(p_w, axis=1, keepdims=True), (rg, LANE))
        o_groups.append(pv(p_w.astype(BF16), wkv) / l_w)
    o_w = o_groups[0] if nsplit == 1 else jnp.concatenate(o_groups, axis=0)

    def gcol(c):
        return jnp.concatenate([gate[:, 3 * h + c:3 * h + c + 1] for h in range(NSA_HEADS)], axis=0)

    return gcol(0) * o_c + gcol(1) * o_s + gcol(2) * o_w


def _nsa_prompt_body(q_ref, ckv_ref, skv_ref, wkv_ref, gate_ref, o_ref, m_ref, l_ref, acc_ref, *, tq, tk, nsplit):
    qi = pl.program_id(1)
    r = NSA_HEADS * tq
    q = q_ref[...].reshape(r, LANE)
    tpos = qi * tq + lax.broadcasted_iota(jnp.int32, (tq, 1), 0)
    nchunks = (qi * tq + tq + tk - 1) // tk
    wkv = wkv_ref[0, pl.ds(pl.multiple_of(qi * tq, tq), WINDOW + tq), :]

    def get_chunk(c):
        return skv_ref[pl.ds(pl.multiple_of(c * tk, tk), tk), :]

    bpc = tk // SEL_BLOCK
    em = lax.broadcasted_iota(jnp.int32, (LANE, tk), 0)
    ek = lax.broadcasted_iota(jnp.int32, (LANE, tk), 1) >> 6

    def get_expand(c):
        return jnp.where(em == ek + c * bpc, 1.0, 0.0).astype(BF16)

    o = _nsa_core(q, tpos, ckv_ref[0], get_chunk, get_expand, nchunks, tk, wkv, qi * tq - WINDOW, gate_ref[...],
                  m_ref, l_ref, acc_ref, tq=tq, n_cmp=SEQ // CMP_BLOCK, n_sel=SEQ // SEL_BLOCK, nsl=LANE,
                  keys_on_lanes=False, nsplit=nsplit)
    o_ref[...] = o.reshape(NSA_HEADS, tq, LANE)


def _nsa_prompt(nq, ckv, rowsb, wpad, gate, *, tq=128, tk=512, nsplit=2):
    nqt = SEQ // tq
    r = NSA_HEADS * tq
    return pl.pallas_call(
        functools.partial(_nsa_prompt_body, tq=tq, tk=tk, nsplit=nsplit),
        grid=(BATCH, nqt),
        in_specs=[
            pl.BlockSpec((NSA_HEADS, tq, LANE), lambda b, i: (0, b * nqt + i, 0)),
            pl.BlockSpec((1, 2 * LANE, LANE), lambda b, i: (b, 0, 0)),
            pl.BlockSpec((SEQ, LANE), lambda b, i: (b, 1)),
            pl.BlockSpec((1, WINDOW + SEQ, LANE), lambda b, i: (b, 0, 0)),
            pl.BlockSpec((tq, LANE), lambda b, i: (b * nqt + i, 0)),
        ],
        out_specs=pl.BlockSpec((NSA_HEADS, tq, LANE), lambda b, i: (0, b * nqt + i, 0)),
        out_shape=jax.ShapeDtypeStruct((NSA_HEADS, MP, LANE), F32),
        scratch_shapes=[pltpu.VMEM((r, LANE), F32), pltpu.VMEM((r, LANE), F32), pltpu.VMEM((r, LANE), F32)],
        compiler_params=_params(("arbitrary", "arbitrary")),
        name="nsa_prompt",
    )(nq, ckv, rowsb, wpad, gate)


_SAMPLE_LK = PAST_LEN + PAGE_SIZE


def _nsa_sample_body(pt_ref, q_ref, snew_ref, win_ref, wnew_ref, gate_ref, exp_ref,
                     perm_ref, pe_ref, w1_ref, w2_ref, peo_ref, cache_ref,
                     o_ref, pages_ref, sem_ref, g_ref, skv_ref, wkv_ref, m_ref, l_ref, acc_ref, *, layer):
    slot = _paged_prefetch(pt_ref, cache_ref, pages_ref, sem_ref, layer)
    half = 2 * NSA_HD
    for pq in range(N_PAGES // 4):
        xt = jnp.concatenate(
            [jnp.concatenate([pages_ref[slot, 4 * pq + 2 * a, 0:half, :],
                              pages_ref[slot, 4 * pq + 2 * a + 1, 0:half, :]], axis=1) for a in range(2)],
            axis=0).astype(BF16)
        gall = _dot_nt(perm_ref[...], xt)
        _scatter_pair(gall[:, 0:half], g_ref, 2 * pq)
        _scatter_pair(gall[:, half:2 * half], g_ref, 2 * pq + 1)
    ckv = _compress_finish(g_ref, pe_ref, w1_ref, w2_ref, peo_ref, nb=PAST_LEN // CMP_BLOCK)
    for i in range(N_PAGES):
        skv_ref[:, i * PAGE_SIZE:(i + 1) * PAGE_SIZE] = pages_ref[slot, i, half:2 * half, :].astype(BF16)
    skv_ref[:, PAST_LEN:_SAMPLE_LK] = snew_ref[0]
    wkv_ref[:, 0:WINDOW] = win_ref[...].astype(BF16)
    wkv_ref[:, WINDOW:WINDOW + PAGE_SIZE] = wnew_ref[0]
    tpos = PAST_LEN + lax.broadcasted_iota(jnp.int32, (TS, 1), 0)
    n_sel = -(-(PAST_LEN + DEC_SEQ) // SEL_BLOCK)

    o = _nsa_core(q_ref[0], tpos, ckv, lambda c: skv_ref[...], lambda c: exp_ref[...], 1, _SAMPLE_LK,
                  wkv_ref[...], PAST_LEN - WINDOW, gate_ref[0],
                  m_ref, l_ref, acc_ref, tq=TS, n_cmp=PAST_LEN // CMP_BLOCK, n_sel=n_sel, nsl=2 * LANE,
                  keys_on_lanes=True, nsplit=1)
    o_ref[0] = o
    _paged_drain(cache_ref, pages_ref, sem_ref, layer, slot)


def _sel_expand(nsl, lk):
    return (jnp.arange(nsl)[:, None] == (jnp.arange(lk) // SEL_BLOCK)[None, :]).astype(BF16)


def _nsa_sample(page_flat, qs, snew_t, win_state_t, wnew_t, gates, expand, perm, pe, w1, w2, peo,
                cache_nsa_t, layer):
    r = NSA_HEADS * TS
    nb = PAST_LEN // CMP_BLOCK
    b3 = lambda b, pt: (b, 0, 0)
    c2 = lambda b, pt: (0, 0)
    grid_spec = pltpu.PrefetchScalarGridSpec(
        num_scalar_prefetch=1,
        grid=(DEC_BATCH,),
        in_specs=[
            pl.BlockSpec((1, r, LANE), b3),
            pl.BlockSpec((1, LANE, PAGE_SIZE), b3),
            pl.BlockSpec((None, None, LANE, WINDOW), lambda b, pt: (layer, b, 0, 0)),
            pl.BlockSpec((1, LANE, PAGE_SIZE), b3),
            pl.BlockSpec((1, TS, LANE), b3),
            pl.BlockSpec((2 * LANE, _SAMPLE_LK), c2),
            pl.BlockSpec((_PAIR, _PAIR), c2),
            pl.BlockSpec((CMP_BLOCK // 2, 2 * LANE), c2),
            pl.BlockSpec((CMP_BLOCK // 2, 2 * LANE, 2 * CMP_HIDDEN), lambda b, pt: (0, 0, 0)),
            pl.BlockSpec((2 * CMP_HIDDEN, LANE), c2),
            pl.BlockSpec((_NCL, nb), c2),
            pl.BlockSpec(memory_space=pl.ANY),
        ],
        out_specs=pl.BlockSpec((1, r, LANE), b3),
        scratch_shapes=[
            pltpu.VMEM((2, N_PAGES, 4 * NSA_HD, PAGE_SIZE), F32),
            pltpu.SemaphoreType.DMA((2,)),
            pltpu.VMEM((CMP_BLOCK // 2, nb + 8, 2 * LANE), F32),
            pltpu.VMEM((LANE, _SAMPLE_LK), BF16),
            pltpu.VMEM((LANE, WINDOW + PAGE_SIZE), BF16),
            pltpu.VMEM((r, LANE), F32), pltpu.VMEM((r, LANE), F32), pltpu.VMEM((r, LANE), F32),
        ],
    )
    return pl.pallas_call(
        functools.partial(_nsa_sample_body, layer=layer),
        grid_spec=grid_spec,
        out_shape=jax.ShapeDtypeStruct((DEC_BATCH, r, LANE), F32),
        compiler_params=_params(("arbitrary",)),
        name="nsa_sample",
    )(page_flat, qs, snew_t, win_state_t, wnew_t, gates, expand, perm, pe, w1, w2, peo, cache_nsa_t)


_CONV_PAD = 32
CONV_PAD_W = 32


def _conv_rows(xbuf, w, bias, g, beta, t_out):
    acc = jnp.zeros((t_out, CONV_CH), F32) + bias
    off = _CONV_PAD - (CONV_W - 1)
    for k in range(CONV_W):
        acc = acc + xbuf[off + k:off + k + t_out, :] * w[k:k + 1, :]
    mu = jnp.mean(acc, axis=-1, keepdims=True)
    var = jnp.mean(jnp.square(acc - mu), axis=-1, keepdims=True)
    y = (acc - mu) * lax.rsqrt(var + EPS) * g + beta
    return y * jax.nn.sigmoid(y)


def _conv_prompt_body(main_ref, halo_ref, w_ref, b_ref, g_ref, beta_ref, o_ref, *, tt):
    xbuf = jnp.concatenate([main_ref[0], halo_ref[0]], axis=0)
    o_ref[0] = _conv_rows(xbuf, w_ref[...], b_ref[...], g_ref[...], beta_ref[...], tt)


def _conv_prompt(buf, w, b, g, beta, *, tt=256):
    nt = SEQ // tt
    vec = lambda bb, j: (0, 0)
    return pl.pallas_call(
        functools.partial(_conv_prompt_body, tt=tt),
        grid=(BATCH, nt),
        in_specs=[
            pl.BlockSpec((1, tt, CONV_CH), lambda bb, j: (bb, j, 0)),
            pl.BlockSpec((1, _CONV_PAD, CONV_CH), lambda bb, j: (bb, (j + 1) * (tt // _CONV_PAD), 0)),
            pl.BlockSpec((CONV_PAD_W, CONV_CH), vec),
            pl.BlockSpec((1, CONV_CH), vec),
            pl.BlockSpec((1, CONV_CH), vec),
            pl.BlockSpec((1, CONV_CH), vec),
        ],
        out_specs=pl.BlockSpec((1, tt, CONV_CH), lambda bb, j: (bb, j, 0)),
        out_shape=jax.ShapeDtypeStruct((BATCH, SEQ, CONV_CH), F32),
        compiler_params=_params(("arbitrary", "arbitrary")),
        name="conv_prompt",
    )(buf, buf, w, b, g, beta)


def _conv_sample_body(buf_ref, w_ref, b_ref, g_ref, beta_ref, o_ref, *, nb):
    for i in range(nb):
        o_ref[i] = _conv_rows(buf_ref[i], w_ref[...], b_ref[...], g_ref[...], beta_ref[...], TS)


def _conv_sample(buf, w, b, g, beta, *, nb=16):
    rows = buf.shape[1]
    vec = lambda i: (0, 0)
    return pl.pallas_call(
        functools.partial(_conv_sample_body, nb=nb),
        grid=(DEC_BATCH // nb,),
        in_specs=[
            pl.BlockSpec((nb, rows, CONV_CH), lambda i: (i, 0, 0)),
            pl.BlockSpec((CONV_PAD_W, CONV_CH), vec),
            pl.BlockSpec((1, CONV_CH), vec),
            pl.BlockSpec((1, CONV_CH), vec),
            pl.BlockSpec((1, CONV_CH), vec),
        ],
        out_specs=pl.BlockSpec((nb, TS, CONV_CH), lambda i: (i, 0, 0)),
        out_shape=jax.ShapeDtypeStruct((DEC_BATCH, TS, CONV_CH), F32),
        compiler_params=_params(("arbitrary",)),
        name="conv_sample",
    )(buf, w, b, g, beta)


def _merge_body(x_ref, olp_ref, ols_ref, onp_ref, ons_ref, ocp_ref, ocs_ref, wuv_ref, gn_ref, wo_ref, o_ref, *, np_tiles):
    is_sample = pl.program_id(0) == np_tiles
    ol = jnp.where(is_sample, ols_ref[...], olp_ref[...])
    on = jnp.where(is_sample, ons_ref[...], onp_ref[...])
    oc = jnp.where(is_sample, ocs_ref[...], ocp_ref[...])
    om = jnp.concatenate(
        [_dot(ol[:, h * MLA_KV_RANK:(h + 1) * MLA_KV_RANK], wuv_ref[h]) for h in range(MLA_HEADS)], axis=1)
    a1 = MLA_WIDTH
    a2 = MLA_WIDTH + NSA_WIDTH
    parts = jnp.concatenate([
        _rms(om, gn_ref[:, 0:a1]).astype(BF16),
        _rms(on, gn_ref[:, a1:a2]).astype(BF16),
        _rms(oc, gn_ref[:, a2:D_MODEL]).astype(BF16),
    ], axis=1)
    o_ref[...] = x_ref[...] + _dot(parts, wo_ref[...])


def _merge(x, ol_p, ol_s, on_p, on_s, oc_p, oc_s, wuv, gn, wo_all, layer, *, tm=MS):
    m = x.shape[0]
    np_tiles = MP // tm
    row = lambda i: (i, 0)
    prow = lambda i: (jnp.minimum(i, np_tiles - 1), 0)
    srow = lambda i: (0, 0)
    return pl.pallas_call(
        functools.partial(_merge_body, np_tiles=np_tiles),
        grid=(m // tm,),
        in_specs=[
            pl.BlockSpec((tm, D_MODEL), row),
            pl.BlockSpec((tm, MLA_HEADS * MLA_KV_RANK), prow),
            pl.BlockSpec((tm, MLA_HEADS * MLA_KV_RANK), srow),
            pl.BlockSpec((tm, NSA_WIDTH), prow),
            pl.BlockSpec((tm, NSA_WIDTH), srow),
            pl.BlockSpec((tm, CONV_CH), prow),
            pl.BlockSpec((tm, CONV_CH), srow),
            pl.BlockSpec((MLA_HEADS, MLA_KV_RANK, MLA_V), lambda i: (0, 0, 0)),
            pl.BlockSpec((1, D_MODEL), lambda i: (0, 0)),
            pl.BlockSpec((None, D_MODEL, D_MODEL), lambda i: (layer, 0, 0)),
        ],
        out_specs=pl.BlockSpec((tm, D_MODEL), row),
        out_shape=jax.ShapeDtypeStruct((m, D_MODEL), F32),
        compiler_params=_params(("arbitrary",)),
        name="merge",
    )(x, ol_p, ol_s, on_p, on_s, oc_p, oc_s, wuv, gn, wo_all)


def _mlp_body(x_ref, g_ref, w1_ref, w2_ref, o_ref, xn_ref, acc_ref):
    j = pl.program_id(1)

    @pl.when(j == 0)
    def _():
        xn_ref[...] = _rms(x_ref[...], g_ref[...]).astype(BF16)
        acc_ref[...] = jnp.zeros_like(acc_ref)

    h = jnp.maximum(_dot(xn_ref[...], w1_ref[...]), 0.0)
    acc_ref[...] += _dot((h * h).astype(BF16), w2_ref[...])

    @pl.when(j == pl.num_programs(1) - 1)
    def _():
        o_ref[...] = x_ref[...] + acc_ref[...]


def _mlp(x, g, w1_all, w2_all, layer, *, tm=512, tf=1024):
    m = x.shape[0]
    return pl.pallas_call(
        _mlp_body,
        grid=(m // tm, D_FF // tf),
        in_specs=[
            pl.BlockSpec((tm, D_MODEL), lambda i, j: (i, 0)),
            pl.BlockSpec((1, D_MODEL), lambda i, j: (0, 0)),
            pl.BlockSpec((None, D_MODEL, tf), lambda i, j: (layer, 0, j)),
            pl.BlockSpec((None, tf, D_MODEL), lambda i, j: (layer, j, 0)),
        ],
        out_specs=pl.BlockSpec((tm, D_MODEL), lambda i, j: (i, 0)),
        out_shape=jax.ShapeDtypeStruct((m, D_MODEL), F32),
        scratch_shapes=[pltpu.VMEM((tm, D_MODEL), BF16), pltpu.VMEM((tm, D_MODEL), F32)],
        compiler_params=_params(("arbitrary", "arbitrary")),
        name="mlp",
    )(x, g, w1_all, w2_all)


def _final_norm_body(x_ref, g_ref, o_ref):
    o_ref[...] = _rms(x_ref[...], g_ref[...])


def _final_norm(x, g, *, tm=512):
    m = x.shape[0]
    return pl.pallas_call(
        _final_norm_body,
        grid=(m // tm,),
        in_specs=[pl.BlockSpec((tm, D_MODEL), lambda i: (i, 0)), pl.BlockSpec((1, D_MODEL), lambda i: (0, 0))],
        out_specs=pl.BlockSpec((tm, D_MODEL), lambda i: (i, 0)),
        out_shape=jax.ShapeDtypeStruct((m, D_MODEL), F32),
        compiler_params=_params(("arbitrary",)),
        name="final_norm",
    )(x, g)


def _pad_rows(w, n):
    return jnp.pad(w, ((0, n - w.shape[0]), (0, 0)))


def _pack_w_in(w_t):
    segs = [
        w_t[OFF_CQ:OFF_CKV],
        w_t[OFF_CKV:OFF_KR],
        _pad_rows(w_t[OFF_KR:OFF_NQ], LANE),
        w_t[OFF_NQ:OFF_NKV],
        w_t[OFF_NKV:OFF_GATE],
        _pad_rows(w_t[OFF_GATE:OFF_CONV], LANE),
        w_t[OFF_CONV:IN_COLS],
    ]
    return jnp.concatenate(segs, axis=0).astype(BF16)


def _pack_w_q_up(w):
    w = w.reshape(MLA_Q_RANK, MLA_HEADS, MLA_NOPE + MLA_ROPE)
    nope = w[:, :, :MLA_NOPE].reshape(MLA_Q_RANK, MLA_HEADS * MLA_NOPE)
    rope = w[:, :, MLA_NOPE:].reshape(MLA_Q_RANK, MLA_HEADS * MLA_ROPE)
    return jnp.concatenate([nope, rope], axis=1).astype(BF16)


def _pack_cmp(pe, w1, w2):
    pe_t = jnp.concatenate([pe[0], pe[1]], axis=1)
    w1k = w1[0].reshape(CMP_BLOCK, NSA_HD, CMP_HIDDEN)
    w1v = w1[1].reshape(CMP_BLOCK, NSA_HD, CMP_HIDDEN)
    zero = jnp.zeros_like(w1k)
    w1b = jnp.concatenate([jnp.concatenate([w1k, zero], axis=2),
                           jnp.concatenate([zero, w1v], axis=2)], axis=1)
    z2 = jnp.zeros_like(w2[0])
    w2b = jnp.concatenate([jnp.concatenate([w2[0], z2], axis=1),
                           jnp.concatenate([z2, w2[1]], axis=1)], axis=0)
    pe2 = pe_t.reshape(CMP_BLOCK // 2, 2 * LANE)
    w1p = w1b.reshape(CMP_BLOCK // 2, 2 * LANE, 2 * CMP_HIDDEN)
    return pe2, w1p.astype(BF16), w2b.astype(BF16)


def _rope_tables():
    half = NSA_HD // 2
    freq = ROPE_THETA ** (-jnp.arange(half, dtype=F32) / half)
    pos = jnp.concatenate([jnp.arange(SEQ, dtype=jnp.int32),
                           PAST_LEN + (jnp.arange(256, dtype=jnp.int32) % DEC_SEQ)])
    ang = pos.astype(F32)[:, None] * freq[None, :]
    cos = jnp.tile(jnp.cos(ang), (1, LANE // half))
    sin = jnp.tile(jnp.sin(ang), (1, LANE // half))
    return cos, sin


def kernel(x_prompt, x_sample, cache_mla, cache_nsa, state_nsa_win, state_conv, page_table, norm_mix, w_in, mla_q_norm, mla_kv_norm, mla_w_q_up, mla_w_uk, mla_w_uv, nsa_cmp_pe, nsa_cmp_w1, nsa_cmp_w2, conv_w, conv_b, conv_ln_g, conv_ln_b, group_norm, w_out, norm_mlp, mlp_w1, mlp_w2, norm_final):
    x = jnp.concatenate([x_prompt.reshape(MP, D_MODEL), x_sample.reshape(MS, D_MODEL)], axis=0)
    cos_t, sin_t = _rope_tables()
    page_flat = page_table.reshape(-1).astype(jnp.int32)
    n_pool = cache_nsa.shape[1]
    cache_mla_t = jnp.swapaxes(cache_mla, 2, 3)
    cache_nsa_t = jnp.transpose(cache_nsa, (0, 1, 3, 4, 5, 2)).reshape(DEPTH, n_pool, 4 * NSA_HD, PAGE_SIZE)
    win_t = jnp.transpose(state_nsa_win, (0, 1, 3, 4, 5, 2)).reshape(DEPTH, DEC_BATCH, 2 * NSA_HD, WINDOW)
    perm, peo_p = _perm_matrices(SEQ // CMP_BLOCK)
    _, peo_s = _perm_matrices(PAST_LEN // CMP_BLOCK)
    sel_expand = _sel_expand(2 * LANE, _SAMPLE_LK)
    w_out_b, mlp_w1_b, mlp_w2_b = w_out.astype(BF16), mlp_w1.astype(BF16), mlp_w2.astype(BF16)
    w_in_t = jnp.transpose(w_in, (2, 0, 1))

    mla_p, mla_s, nsa_p, nsa_s, win_p, win_s, conv_p, conv_s = [], [], [], [], [], [], [], []
    for l in range(DEPTH):
        wp = _pack_w_in(w_in_t[:, l, :])
        wq = _pack_w_q_up(mla_w_q_up[l])
        wuk = jnp.transpose(mla_w_uk[l], (1, 2, 0)).astype(BF16)
        wuv = jnp.transpose(mla_w_uv[l], (1, 0, 2)).astype(BF16)
        pe_t, w1b, w2b = _pack_cmp(nsa_cmp_pe[l], nsa_cmp_w1[l], nsa_cmp_w2[l])
        cw = jnp.pad(conv_w[l], ((0, CONV_PAD_W - CONV_W), (0, 0)))

        z = _inproj(x, norm_mix[l][None], wp)
        q384, kvb, mla_row, nq, rows, rowsb, gate, u = _post(
            z, cos_t, sin_t, mla_q_norm[l][None], mla_kv_norm[l][None], wq, wuk)

        mla_p.append(mla_row[:MP, :MLA_ROW].reshape(BATCH, SEQ, MLA_ROW))
        mla_s.append(mla_row[MP:, :MLA_ROW].reshape(DEC_BATCH, DEC_SEQ, MLA_ROW))
        nsa_p.append(rows[:MP, :4 * NSA_HD].reshape(BATCH, SEQ, 4, 1, NSA_HD))
        nsa_s.append(rows[MP:, :4 * NSA_HD].reshape(DEC_BATCH, DEC_SEQ, 4, 1, NSA_HD))
        win_rows_p = rows[:MP, 4 * NSA_HD:].reshape(BATCH, SEQ, 2, 1, NSA_HD)
        win_p.append(win_rows_p[:, SEQ - WINDOW:])
        win_rows_s = rows[MP:, 4 * NSA_HD:].reshape(DEC_BATCH, DEC_SEQ, 2, 1, NSA_HD)
        win_s.append(jnp.concatenate([state_nsa_win[l][:, DEC_SEQ:], win_rows_s], axis=1))
        up = u[:MP].reshape(BATCH, SEQ, CONV_CH)
        us = u[MP:].reshape(DEC_BATCH, DEC_SEQ, CONV_CH)
        bufs = jnp.concatenate([state_conv[l], us], axis=1)
        conv_p.append(up[:, SEQ - (CONV_W - 1):])
        conv_s.append(bufs[:, DEC_SEQ:])

        ol_p = _mla_prompt(q384, kvb)
        qs = q384[:, MP:].reshape(MLA_HEADS, DEC_BATCH, DEC_SEQ, KV_PAD)
        qs = jnp.transpose(qs, (1, 0, 2, 3)).reshape(DEC_BATCH, MLA_HEADS * DEC_SEQ, KV_PAD)
        knew = jnp.pad(kvb[MP:].reshape(DEC_BATCH, DEC_SEQ, KV_PAD), ((0, 0), (0, PAGE_SIZE - DEC_SEQ), (0, 0)))
        ol_s = _mla_sample(page_flat, qs, jnp.swapaxes(knew, 1, 2), cache_mla_t, l)
        ol_s = ol_s.reshape(DEC_BATCH, MLA_HEADS, DEC_SEQ, MLA_KV_RANK)
        ol_s = jnp.transpose(ol_s, (0, 2, 1, 3)).reshape(MS, MLA_HEADS * MLA_KV_RANK)

        ckv_p = _compress_prompt(rows, perm, pe_t, w1b, w2b, peo_p)
        wpad = jnp.pad(rowsb[:MP, 2 * LANE:].reshape(BATCH, SEQ, LANE), ((0, 0), (WINDOW, 0), (0, 0)))
        on_p = _nsa_prompt(nq, ckv_p, rowsb, wpad, gate)
        on_p = jnp.transpose(on_p[:, :, NSA_HD:], (1, 0, 2)).reshape(MP, NSA_WIDTH)

        pad_tok = ((0, 0), (0, 0), (0, TS - DEC_SEQ), (0, 0))
        nqs = jnp.pad(nq[:, MP:].reshape(NSA_HEADS, DEC_BATCH, DEC_SEQ, LANE), pad_tok)
        nqs = jnp.transpose(nqs, (1, 0, 2, 3)).reshape(DEC_BATCH, NSA_HEADS * TS, LANE)
        pad_rows = ((0, 0), (0, PAGE_SIZE - DEC_SEQ), (0, 0))
        snew = jnp.pad(rowsb[MP:, LANE:2 * LANE].reshape(DEC_BATCH, DEC_SEQ, LANE), pad_rows)
        wnew = jnp.pad(rowsb[MP:, 2 * LANE:].reshape(DEC_BATCH, DEC_SEQ, LANE), pad_rows)
        gs = jnp.pad(gate[MP:].reshape(DEC_BATCH, DEC_SEQ, LANE), ((0, 0), (0, TS - DEC_SEQ), (0, 0)))
        on_s = _nsa_sample(page_flat, nqs, jnp.swapaxes(snew, 1, 2), win_t, jnp.swapaxes(wnew, 1, 2),
                           gs, sel_expand, perm, pe_t, w1b, w2b, peo_s, cache_nsa_t, l)
        on_s = on_s[:, :, NSA_HD:].reshape(DEC_BATCH, NSA_HEADS, TS, NSA_HD)[:, :, :DEC_SEQ]
        on_s = jnp.transpose(on_s, (0, 2, 1, 3)).reshape(MS, NSA_WIDTH)

        bufp = jnp.pad(up, ((0, 0), (_CONV_PAD, 0), (0, 0)))
        oc_p = _conv_prompt(bufp, cw, conv_b[l][None], conv_ln_g[l][None], conv_ln_b[l][None])
        bufs_p = jnp.pad(bufs, ((0, 0), (_CONV_PAD - (CONV_W - 1), TS - DEC_SEQ), (0, 0)))
        oc_s = _conv_sample(bufs_p, cw, conv_b[l][None], conv_ln_g[l][None], conv_ln_b[l][None])
        oc_p = oc_p.reshape(MP, CONV_CH)
        oc_s = oc_s[:, :DEC_SEQ].reshape(MS, CONV_CH)

        x = _merge(x, ol_p, ol_s, on_p, on_s, oc_p, oc_s, wuv, group_norm[l][None], w_out_b, l)
        x = _mlp(x, norm_mlp[l][None], mlp_w1_b, mlp_w2_b, l)

    y = _final_norm(x, norm_final[None])
    return (y[:MP].reshape(BATCH, SEQ, D_MODEL), y[MP:].reshape(DEC_BATCH, DEC_SEQ, D_MODEL),
            jnp.stack(mla_p), jnp.stack(mla_s), jnp.stack(nsa_p), jnp.stack(nsa_s),
            jnp.stack(win_p), jnp.stack(win_s), jnp.stack(conv_p), jnp.stack(conv_s))
```

```python
import functools
import math

import jax
import jax.numpy as jnp
from jax import lax
from jax.experimental import pallas as pl
from jax.experimental.pallas import tpu as pltpu

F32 = jnp.float32
BF16 = jnp.bfloat16

D_MODEL = 2048
BATCH = 4
SEQ = 2048
DEPTH = 4
DEC_BATCH = 128
DEC_SEQ = 4
PAST_LEN = 8192
PAGE_SIZE = 128
N_PAGES = PAST_LEN // PAGE_SIZE
MLA_NOPE = 128
MLA_ROPE = 64
MLA_V = 128
MLA_HEADS = 6
MLA_Q_RANK = 512
MLA_KV_RANK = 256
MLA_ROW = MLA_KV_RANK + MLA_ROPE
MLA_WIDTH = MLA_HEADS * MLA_V
MLA_SCALE = 1.0 / math.sqrt(MLA_NOPE + MLA_ROPE)
NSA_HD = 64
NSA_HEADS = 12
NSA_WIDTH = NSA_HEADS * NSA_HD
NSA_SCALE = 1.0 / math.sqrt(NSA_HD)
CMP_BLOCK = 32
CMP_HIDDEN = 128
SEL_BLOCK = 64
SEL_TOPK = 16
WINDOW = 512
SEL_FORCE = 1e4
CONV_CH = 512
CONV_W = 31
D_FF = 4 * D_MODEL
ROPE_THETA = 10000.0
EPS = 1e-6

OFF_CQ = 0
OFF_CKV = OFF_CQ + MLA_Q_RANK
OFF_KR = OFF_CKV + MLA_KV_RANK
OFF_NQ = OFF_KR + MLA_ROPE
OFF_NKV = OFF_NQ + NSA_WIDTH
OFF_GATE = OFF_NKV + 6 * NSA_HD
OFF_CONV = OFF_GATE + 3 * NSA_HEADS
IN_COLS = OFF_CONV + 2 * CONV_CH

LANE = 128
VMEM_LIMIT = 56 * 1024 * 1024

MP = BATCH * SEQ
MS = DEC_BATCH * DEC_SEQ
MT = MP + MS
TS = 8
KV_PAD = 3 * LANE

Z_CQ = 0
Z_CKV = 512
Z_KR = 768
Z_NQ = 896
Z_NKV = 1664
Z_GATE = 2048
Z_CA = 2176
Z_CB = 2688
NZ = 3200

NEG = -1e30


def _dot(a, b):
    return jnp.dot(a, b, preferred_element_type=F32)


def _dot_nt(a, b):
    return lax.dot_general(a, b, (((1,), (1,)), ((), ())), preferred_element_type=F32)


def _rms(x, g):
    return x * lax.rsqrt(jnp.mean(x * x, axis=-1, keepdims=True) + EPS) * g


def _rep(x, n):
    return x if n == 1 else jnp.concatenate([x] * n, axis=1)


def _params(sem):
    return pltpu.CompilerParams(dimension_semantics=sem, vmem_limit_bytes=VMEM_LIMIT)


def _inproj_body(x_ref, g_ref, w_ref, o_ref, xn_ref):
    @pl.when(pl.program_id(1) == 0)
    def _():
        xn_ref[...] = _rms(x_ref[...], g_ref[...]).astype(BF16)

    o_ref[...] = _dot_nt(xn_ref[...], w_ref[...])


def _inproj(x, g, w_t, *, tm=512, tn=3200):
    m = x.shape[0]
    return pl.pallas_call(
        _inproj_body,
        grid=(m // tm, NZ // tn),
        in_specs=[
            pl.BlockSpec((tm, D_MODEL), lambda i, j: (i, 0)),
            pl.BlockSpec((1, D_MODEL), lambda i, j: (0, 0)),
            pl.BlockSpec((tn, D_MODEL), lambda i, j: (j, 0)),
        ],
        out_specs=pl.BlockSpec((tm, tn), lambda i, j: (i, j)),
        out_shape=jax.ShapeDtypeStruct((m, NZ), F32),
        scratch_shapes=[pltpu.VMEM((tm, D_MODEL), BF16)],
        compiler_params=_params(("arbitrary", "arbitrary")),
        name="inproj",
    )(x, g, w_t)


def _post_body(z_ref, cos_ref, sin_ref, qn_ref, kvn_ref, wq_ref, wuk_ref,
               q_ref, kvb_ref, mla_ref, nq_ref, rows_ref, rowsb_ref, gate_ref, u_ref, *, tm):
    cos = cos_ref[...]
    sin = sin_ref[...]
    lane = lax.broadcasted_iota(jnp.int32, (tm, LANE), 1)
    first = (lane & 63) < 32
    lo = lane < 64
    sa = jnp.where(first, -sin, 0.0)
    sb = jnp.where(first, 0.0, sin)
    ch = jnp.where(lo, cos, 1.0)
    sah = jnp.where(lo, sa, 0.0)
    sbh = jnp.where(lo, sb, 0.0)

    def rope_full(x):
        return x * cos + pltpu.roll(x, 96, 1) * sa + pltpu.roll(x, 32, 1) * sb

    def rope_half(x):
        return x * ch + pltpu.roll(x, 96, 1) * sah + pltpu.roll(x, 32, 1) * sbh

    def split_heads(t):
        return jnp.where(lo, t, 0.0), jnp.where(lo, pltpu.roll(t, 64, 1), 0.0)

    cq = _rms(z_ref[:, Z_CQ:Z_CQ + MLA_Q_RANK], qn_ref[...]).astype(BF16)
    q = _dot(cq, wq_ref[...])
    for h in range(MLA_HEADS):
        ql = _dot(q[:, h * MLA_NOPE:(h + 1) * MLA_NOPE].astype(BF16), wuk_ref[h])
        q_ref[h, :, 0:MLA_KV_RANK] = (ql * MLA_SCALE).astype(BF16)
    rope0 = MLA_HEADS * MLA_NOPE
    for j in range(MLA_HEADS // 2):
        t = rope_full(q[:, rope0 + j * LANE:rope0 + (j + 1) * LANE]) * MLA_SCALE
        a, b = split_heads(t)
        q_ref[2 * j, :, MLA_KV_RANK:KV_PAD] = a.astype(BF16)
        q_ref[2 * j + 1, :, MLA_KV_RANK:KV_PAD] = b.astype(BF16)

    ckv = _rms(z_ref[:, Z_CKV:Z_CKV + MLA_KV_RANK], kvn_ref[...])
    kr = rope_half(z_ref[:, Z_KR:Z_KR + LANE])
    mla_ref[:, 0:MLA_KV_RANK] = ckv
    mla_ref[:, MLA_KV_RANK:KV_PAD] = kr
    kvb_ref[:, 0:MLA_KV_RANK] = ckv.astype(BF16)
    kvb_ref[:, MLA_KV_RANK:KV_PAD] = kr.astype(BF16)

    for j in range(NSA_HEADS // 2):
        t = rope_full(z_ref[:, Z_NQ + j * LANE:Z_NQ + (j + 1) * LANE]) * NSA_SCALE
        a, b = split_heads(t)
        nq_ref[2 * j] = a.astype(BF16)
        nq_ref[2 * j + 1] = b.astype(BF16)

    for j in range(3):
        t = rope_half(z_ref[:, Z_NKV + j * LANE:Z_NKV + (j + 1) * LANE])
        rows_ref[:, j * LANE:(j + 1) * LANE] = t
        rowsb_ref[:, j * LANE:(j + 1) * LANE] = t.astype(BF16)

    gate_ref[...] = jax.nn.sigmoid(z_ref[:, Z_GATE:Z_GATE + LANE])
    u_ref[...] = z_ref[:, Z_CA:Z_CA + CONV_CH] * jax.nn.sigmoid(z_ref[:, Z_CB:Z_CB + CONV_CH])


def _post(z, cos_t, sin_t, qn, kvn, wq, wuk, *, tm=256):
    m = z.shape[0]
    n_prompt_tiles = MP // tm
    tiles_per_seq = SEQ // tm

    def tab_map(i):
        return (jnp.where(i < n_prompt_tiles, i % tiles_per_seq, tiles_per_seq), 0)

    row = lambda i: (i, 0)
    const2 = lambda i: (0, 0)
    return pl.pallas_call(
        functools.partial(_post_body, tm=tm),
        grid=(m // tm,),
        in_specs=[
            pl.BlockSpec((tm, NZ), row),
            pl.BlockSpec((tm, LANE), tab_map),
            pl.BlockSpec((tm, LANE), tab_map),
            pl.BlockSpec((1, MLA_Q_RANK), const2),
            pl.BlockSpec((1, MLA_KV_RANK), const2),
            pl.BlockSpec((MLA_Q_RANK, MLA_HEADS * (MLA_NOPE + MLA_ROPE)), const2),
            pl.BlockSpec((MLA_HEADS, MLA_NOPE, MLA_KV_RANK), lambda i: (0, 0, 0)),
        ],
        out_specs=[
            pl.BlockSpec((MLA_HEADS, tm, KV_PAD), lambda i: (0, i, 0)),
            pl.BlockSpec((tm, KV_PAD), row),
            pl.BlockSpec((tm, KV_PAD), row),
            pl.BlockSpec((NSA_HEADS, tm, LANE), lambda i: (0, i, 0)),
            pl.BlockSpec((tm, 3 * LANE), row),
            pl.BlockSpec((tm, 3 * LANE), row),
            pl.BlockSpec((tm, LANE), row),
            pl.BlockSpec((tm, CONV_CH), row),
        ],
        out_shape=[
            jax.ShapeDtypeStruct((MLA_HEADS, m, KV_PAD), BF16),
            jax.ShapeDtypeStruct((m, KV_PAD), BF16),
            jax.ShapeDtypeStruct((m, KV_PAD), F32),
            jax.ShapeDtypeStruct((NSA_HEADS, m, LANE), BF16),
            jax.ShapeDtypeStruct((m, 3 * LANE), F32),
            jax.ShapeDtypeStruct((m, 3 * LANE), BF16),
            jax.ShapeDtypeStruct((m, LANE), F32),
            jax.ShapeDtypeStruct((m, CONV_CH), F32),
        ],
        compiler_params=_params(("arbitrary",)),
        name="post_proj",
    )(z, cos_t, sin_t, qn, kvn, wq, wuk)


def _mla_prompt_body(q_ref, kv_ref, o_ref, m_ref, l_ref, acc_ref, *, tq, tk, nsplit):
    qi = pl.program_id(1)
    r = MLA_HEADS * tq
    q = q_ref[...].reshape(r, KV_PAD)
    m_ref[...] = jnp.full((r, LANE), NEG, F32)
    l_ref[...] = jnp.zeros((r, LANE), F32)
    acc_ref[...] = jnp.zeros((r, MLA_KV_RANK), F32)
    rg = r // nsplit
    qpos = qi * tq + (lax.broadcasted_iota(jnp.int32, (rg, tk), 0) & (tq - 1))
    col = lax.broadcasted_iota(jnp.int32, (rg, tk), 1)
    nchunks = (qi * tq + tq + tk - 1) // tk

    def step(j, carry):
        k = kv_ref[pl.ds(pl.multiple_of(j * tk, tk), tk), :]
        for g in range(nsplit):
            rows = slice(g * rg, (g + 1) * rg)
            s = _dot_nt(q[rows], k)
            s = jnp.where(j * tk + col <= qpos, s, NEG)
            m_old = m_ref[rows, :]
            m_new = jnp.maximum(m_old, jnp.max(s, axis=1, keepdims=True))
            alpha = jnp.exp(m_old - m_new)
            p = jnp.exp(s - _rep(m_new, tk // LANE))
            l_ref[rows, :] = alpha * l_ref[rows, :] + jnp.sum(p, axis=1, keepdims=True)
            acc_ref[rows, :] = (_rep(alpha, MLA_KV_RANK // LANE) * acc_ref[rows, :]
                                + _dot(p.astype(BF16), k[:, 0:MLA_KV_RANK]))
            m_ref[rows, :] = m_new
        return carry

    lax.fori_loop(0, nchunks, step, 0)
    o = acc_ref[...] / _rep(l_ref[...], MLA_KV_RANK // LANE)
    for h in range(MLA_HEADS):
        o_ref[:, h * MLA_KV_RANK:(h + 1) * MLA_KV_RANK] = o[h * tq:(h + 1) * tq].astype(BF16)


def _mla_prompt(q, kvb, *, tq=256, tk=512, nsplit=2):
    nq = SEQ // tq
    r = MLA_HEADS * tq
    return pl.pallas_call(
        functools.partial(_mla_prompt_body, tq=tq, tk=tk, nsplit=nsplit),
        grid=(BATCH, nq),
        in_specs=[
            pl.BlockSpec((MLA_HEADS, tq, KV_PAD), lambda b, i: (0, b * nq + i, 0)),
            pl.BlockSpec((SEQ, KV_PAD), lambda b, i: (b, 0)),
        ],
        out_specs=pl.BlockSpec((tq, MLA_HEADS * MLA_KV_RANK), lambda b, i: (b * nq + i, 0)),
        out_shape=jax.ShapeDtypeStruct((MP, MLA_HEADS * MLA_KV_RANK), BF16),
        scratch_shapes=[pltpu.VMEM((r, LANE), F32), pltpu.VMEM((r, LANE), F32), pltpu.VMEM((r, MLA_KV_RANK), F32)],
        compiler_params=_params(("arbitrary", "arbitrary")),
        name="mla_prompt",
    )(q, kvb)


def _page_copy(cache_ref, pages_ref, sem_ref, layer, page, slot, i):
    return pltpu.make_async_copy(cache_ref.at[layer, page], pages_ref.at[slot, i], sem_ref.at[slot])


def _paged_prefetch(pt_ref, cache_ref, pages_ref, sem_ref, layer):
    ns = pages_ref.shape[0]
    dist = ns - 1
    b = pl.program_id(0)
    nb = pl.num_programs(0)
    slot = lax.rem(b, ns)

    @pl.when(b == 0)
    def _():
        for d in range(dist):
            for i in range(N_PAGES):
                _page_copy(cache_ref, pages_ref, sem_ref, layer, pt_ref[d * N_PAGES + i], d, i).start()

    for i in range(N_PAGES):
        _page_copy(cache_ref, pages_ref, sem_ref, layer, 0, slot, i).wait()
    nxt = lax.rem(b + dist, nb)
    nslot = lax.rem(b + dist, ns)
    for i in range(N_PAGES):
        _page_copy(cache_ref, pages_ref, sem_ref, layer, pt_ref[nxt * N_PAGES + i], nslot, i).start()
    return slot


def _paged_drain(cache_ref, pages_ref, sem_ref, layer, slot):
    ns = pages_ref.shape[0]
    b = pl.program_id(0)

    @pl.when(b == pl.num_programs(0) - 1)
    def _():
        for d in range(1, ns):
            for i in range(N_PAGES):
                _page_copy(cache_ref, pages_ref, sem_ref, layer, 0, lax.rem(b + d, ns), i).wait()


def _mla_sample_body(pt_ref, q_ref, knew_ref, cache_ref, o_ref, pages_ref, sem_ref, kt_ref, *, layer):
    slot = _paged_prefetch(pt_ref, cache_ref, pages_ref, sem_ref, layer)
    lk = PAST_LEN + PAGE_SIZE
    kt_ref[MLA_ROW:KV_PAD, :] = jnp.zeros((KV_PAD - MLA_ROW, lk), BF16)
    for i in range(N_PAGES):
        kt_ref[0:MLA_ROW, i * PAGE_SIZE:(i + 1) * PAGE_SIZE] = pages_ref[slot, i].astype(BF16)
    kt_ref[:, PAST_LEN:lk] = knew_ref[0]
    q = q_ref[0]
    r = q.shape[0]
    s = _dot(q, kt_ref[...])
    col = lax.broadcasted_iota(jnp.int32, (r, lk), 1)
    row = lax.broadcasted_iota(jnp.int32, (r, lk), 0)
    valid = (col < PAST_LEN) | ((col - PAST_LEN) <= (row & (DEC_SEQ - 1)))
    s = jnp.where(valid, s, NEG)
    m = jnp.max(s, axis=1, keepdims=True)
    p = jnp.exp(s - m)
    l = jnp.sum(p, axis=1, keepdims=True)
    o = _dot_nt(p.astype(BF16), kt_ref[0:MLA_KV_RANK, :]) / l
    o_ref[0] = o.astype(BF16)
    _paged_drain(cache_ref, pages_ref, sem_ref, layer, slot)


_MLA_SLOTS = 3


def _mla_sample(page_flat, qs, knew_t, cache_mla_t, layer):
    r = MLA_HEADS * DEC_SEQ
    grid_spec = pltpu.PrefetchScalarGridSpec(
        num_scalar_prefetch=1,
        grid=(DEC_BATCH,),
        in_specs=[
            pl.BlockSpec((1, r, KV_PAD), lambda b, pt: (b, 0, 0)),
            pl.BlockSpec((1, KV_PAD, PAGE_SIZE), lambda b, pt: (b, 0, 0)),
            pl.BlockSpec(memory_space=pl.ANY),
        ],
        out_specs=pl.BlockSpec((1, r, MLA_KV_RANK), lambda b, pt: (b, 0, 0)),
        scratch_shapes=[
            pltpu.VMEM((_MLA_SLOTS, N_PAGES, MLA_ROW, PAGE_SIZE), F32),
            pltpu.SemaphoreType.DMA((_MLA_SLOTS,)),
            pltpu.VMEM((KV_PAD, PAST_LEN + PAGE_SIZE), BF16),
        ],
    )
    return pl.pallas_call(
        functools.partial(_mla_sample_body, layer=layer),
        grid_spec=grid_spec,
        out_shape=jax.ShapeDtypeStruct((DEC_BATCH, r, MLA_KV_RANK), BF16),
        compiler_params=_params(("arbitrary",)),
        name="mla_sample",
    )(page_flat, qs, knew_t, cache_mla_t)


def _gelu_tanh(x):
    return 0.5 * x * (1.0 + jnp.tanh(math.sqrt(2.0 / math.pi) * (x + 0.044715 * (x * x * x))))


_PAIR = 2 * PAGE_SIZE
_BPP = _PAIR // CMP_BLOCK
_NCL = 2 * LANE


def _scatter_pair(gall, g_ref, pp):
    for r in range(CMP_BLOCK):
        g_ref[r // 2, pp * _BPP:(pp + 1) * _BPP, (r % 2) * LANE:(r % 2 + 1) * LANE] = gall[r * _BPP:(r + 1) * _BPP, :]


def _compress_finish(g_ref, pe_ref, w1_ref, w2_ref, peo_ref, *, nb):
    nslab = CMP_BLOCK // 2
    for q in range(nslab):
        g_ref[q, nb:nb + 8, :] = jnp.broadcast_to(pe_ref[q:q + 1, :], (8, 2 * LANE))
    h0 = jnp.zeros((nb + 8, 2 * CMP_HIDDEN), F32)
    h1 = jnp.zeros((nb + 8, 2 * CMP_HIDDEN), F32)
    for q in range(0, nslab, 2):
        h0 = h0 + _dot(g_ref[q].astype(BF16), w1_ref[q])
        h1 = h1 + _dot(g_ref[q + 1].astype(BF16), w1_ref[q + 1])
    h = h0 + h1
    hb = h[0:nb] + h[nb:nb + 1]
    c = _dot(_gelu_tanh(hb).astype(BF16), w2_ref[...]).astype(BF16)
    return _dot(peo_ref[...], c).astype(BF16)


def _compress_prompt_body(src_ref, perm_ref, pe_ref, w1_ref, w2_ref, peo_ref, o_ref, g_ref):
    for pp in range(SEQ // _PAIR):
        x = src_ref[pp * _PAIR:(pp + 1) * _PAIR, :].astype(BF16)
        _scatter_pair(_dot(perm_ref[...], x), g_ref, pp)
    o_ref[0] = _compress_finish(g_ref, pe_ref, w1_ref, w2_ref, peo_ref, nb=SEQ // CMP_BLOCK)


def _compress_prompt(rows, perm, pe, w1, w2, peo):
    nb = SEQ // CMP_BLOCK
    c2 = lambda b: (0, 0)
    return pl.pallas_call(
        _compress_prompt_body,
        grid=(BATCH,),
        in_specs=[
            pl.BlockSpec((SEQ, LANE), lambda b: (b, 0)),
            pl.BlockSpec((_PAIR, _PAIR), c2),
            pl.BlockSpec((CMP_BLOCK // 2, 2 * LANE), c2),
            pl.BlockSpec((CMP_BLOCK // 2, 2 * LANE, 2 * CMP_HIDDEN), lambda b: (0, 0, 0)),
            pl.BlockSpec((2 * CMP_HIDDEN, LANE), c2),
            pl.BlockSpec((_NCL, nb), c2),
        ],
        out_specs=pl.BlockSpec((1, _NCL, LANE), lambda b: (b, 0, 0)),
        out_shape=jax.ShapeDtypeStruct((BATCH, _NCL, LANE), BF16),
        scratch_shapes=[pltpu.VMEM((CMP_BLOCK // 2, nb + 8, 2 * LANE), F32)],
        compiler_params=_params(("arbitrary",)),
        name="compress_prompt",
    )(rows, perm, pe, w1, w2, peo)


def _perm_matrices(nb):
    i = jnp.arange(_PAIR)
    perm = (jnp.arange(_PAIR)[None, :] == (CMP_BLOCK * (i % _BPP) + i // _BPP)[:, None]).astype(BF16)
    j = jnp.arange(_NCL)
    blk = jnp.where(j < _NCL // 2, 2 * j, 2 * (j - _NCL // 2) + 1)
    peo = (jnp.arange(nb)[None, :] == blk[:, None]).astype(BF16)
    return perm, peo


def _topk_mask_lanes(score, n_sel, k):
    tq, nsl = score.shape
    mm = lax.broadcasted_iota(jnp.int32, (tq, nsl), 1)
    rank = jnp.zeros((tq, nsl), F32)
    for j in range(n_sel):
        sc = score[:, j:j + 1]
        ahead = (sc > score) | ((sc == score) & (mm > j))
        rank = rank + jnp.where(ahead, 1.0, 0.0)
    return jnp.where(rank < float(k), 1.0, 0.0)


def _topk_mask_sublanes(score, n_sel, k):
    st = score.T
    nsl, tq = st.shape
    mm = lax.broadcasted_iota(jnp.int32, (nsl, tq), 0)
    rank = jnp.zeros((nsl, tq), F32)
    for j in range(n_sel):
        sc = st[j:j + 1, :]
        ahead = (sc > st) | ((sc == st) & (mm > j))
        rank = rank + jnp.where(ahead, 1.0, 0.0)
    return jnp.where(rank < float(k), 1.0, 0.0).T


def _nsa_core(q, tpos, ckv, get_chunk, get_expand, nchunks, tk, wkv, wpos0, gate,
              m_ref, l_ref, acc_ref, *, tq, n_cmp, n_sel, nsl, keys_on_lanes, nsplit):
    if keys_on_lanes:
        def qk(a, kv):
            return _dot(a[:, 0:NSA_HD], kv[0:NSA_HD])

        def pv(p, kv):
            o = _dot_nt(p, kv[NSA_HD:2 * NSA_HD])
            return jnp.concatenate([jnp.zeros_like(o), o], axis=1)
    else:
        qk = _dot_nt
        pv = _dot
    r = NSA_HEADS * tq
    qpos = jnp.concatenate([tpos] * NSA_HEADS, axis=0)

    ncl = ckv.shape[0]
    half = ncl // 2
    s_c = _dot_nt(q, ckv)
    jc = lax.broadcasted_iota(jnp.int32, (1, ncl), 1)
    blk = jnp.where(jc < half, 2 * jc, 2 * (jc - half) + 1)
    valid_c = (blk < n_cmp) & (blk * CMP_BLOCK + (CMP_BLOCK - 1) <= qpos)
    s_c = jnp.where(valid_c, s_c, NEG)
    m_c = jnp.broadcast_to(jnp.max(s_c, axis=1, keepdims=True), (r, LANE))
    p = jnp.where(valid_c, jnp.exp(s_c - _rep(m_c, ncl // LANE)), 0.0)
    d = jnp.broadcast_to(jnp.sum(p, axis=1, keepdims=True), (r, LANE))
    p_c = p / _rep(jnp.where(d > 0, d, 1.0), ncl // LANE)
    o_c = _dot(p_c.astype(BF16), ckv)

    imp = p_c[0:tq]
    for h in range(1, NSA_HEADS):
        imp = imp + p_c[h * tq:(h + 1) * tq]
    imp_sel = imp[:, 0:half] + imp[:, half:ncl]
    if nsl > half:
        imp_sel = jnp.concatenate([imp_sel, jnp.zeros((tq, nsl - half), F32)], axis=1)
    mm = lax.broadcasted_iota(jnp.int32, (tq, nsl), 1)
    cur = tpos >> 6
    valid_s = (mm * SEL_BLOCK <= tpos) & (mm < n_sel)
    forced = (mm == 0) | (mm == cur) | (mm == cur - 1)
    score = jnp.where(valid_s, jnp.where(forced, SEL_FORCE, imp_sel), -jnp.inf)
    topk = min(SEL_TOPK, n_sel)
    if tq == nsl:
        self32 = _topk_mask_sublanes(score, n_sel, topk)
    else:
        self32 = _topk_mask_lanes(score, n_sel, topk)
    selb = self32.astype(BF16)

    def expand_by_broadcast(ntiles):
        lo = lax.broadcasted_iota(jnp.int32, (tq, LANE), 1) < SEL_BLOCK
        tiles = [jnp.where(lo, self32[:, 2 * c:2 * c + 1], self32[:, 2 * c + 1:2 * c + 2]) for c in range(ntiles)]
        return jnp.concatenate(tiles, axis=1) > 0.5

    hg = NSA_HEADS // nsplit
    rg = hg * tq
    kcol = lax.broadcasted_iota(jnp.int32, (tq, tk), 1)

    def step(c, first):
        k = get_chunk(c)
        if get_expand is None:
            chosen = expand_by_broadcast(tk // LANE)
        else:
            chosen = _dot(selb, get_expand(c)) > 0.5
        mask = (chosen & (kcol + c * tk <= tpos))[None]
        for g in range(nsplit):
            rows = slice(g * rg, (g + 1) * rg)
            s = qk(q[rows], k).reshape(hg, tq, tk)
            s = jnp.where(mask, s, NEG).reshape(rg, tk)
            m_cur = jnp.max(s, axis=1, keepdims=True)
            if first:
                m_new = jnp.broadcast_to(m_cur, (rg, LANE))
            else:
                m_old = m_ref[rows, :]
                m_new = jnp.maximum(m_old, m_cur)
                alpha = jnp.exp(m_old - m_new)
            pp = jnp.exp(s - _rep(m_new, tk // LANE))
            psum = jnp.sum(pp, axis=1, keepdims=True)
            acc = pv(pp.astype(BF16), k)
            if first:
                l_ref[rows, :] = jnp.broadcast_to(psum, (rg, LANE))
                acc_ref[rows, :] = acc
            else:
                l_ref[rows, :] = alpha * l_ref[rows, :] + psum
                acc_ref[rows, :] = alpha * acc_ref[rows, :] + acc
            m_ref[rows, :] = m_new

    step(0, True)
    if not (isinstance(nchunks, int) and nchunks == 1):
        def body(c, carry):
            step(c, False)
            return carry
        lax.fori_loop(1, nchunks, body, 0)
    o_s = acc_ref[...] / l_ref[...]

    lw = wkv.shape[1] if keys_on_lanes else wkv.shape[0]
    wpos = wpos0 + lax.broadcasted_iota(jnp.int32, (1, lw), 1)
    dist = tpos - wpos
    mask_w = (dist >= 0) & (dist < WINDOW) & (wpos >= 0)
    o_groups = []
    for g in range(nsplit):
        s_w = qk(q[g * rg:(g + 1) * rg], wkv).reshape(hg, tq, lw)
        s_w = jnp.where(mask_w[None], s_w, NEG).reshape(rg, lw)
        m_w = jnp.broadcast_to(jnp.max(s_w, axis=1, keepdims=True), (rg, LANE))
        p_w = jnp.exp(s_w - _rep(m_w, lw // LANE))
        l_w = jnp.broadcast_to(jnp.sum(p_w, axis=1, keepdims=True), (rg, LANE))
        o_groups.append(pv(p_w.astype(BF16), wkv) / l_w)
    o_w = o_groups[0] if nsplit == 1 else jnp.concatenate(o_groups, axis=0)

    def gcol(c):
        return jnp.concatenate([gate[:, 3 * h + c:3 * h + c + 1] for h in range(NSA_HEADS)], axis=0)

    return gcol(0) * o_c + gcol(1) * o_s + gcol(2) * o_w


def _nsa_prompt_body(q_ref, ckv_ref, skv_ref, wkv_ref, gate_ref, o_ref, m_ref, l_ref, acc_ref, *, tq, tk, nsplit):
    qi = pl.program_id(1)
    r = NSA_HEADS * tq
    q = q_ref[...].reshape(r, LANE)
    tpos = qi * tq + lax.broadcasted_iota(jnp.int32, (tq, 1), 0)
    nchunks = (qi * tq + tq + tk - 1) // tk
    wkv = wkv_ref[0, pl.ds(pl.multiple_of(qi * tq, tq), WINDOW + tq), :]

    def get_chunk(c):
        return skv_ref[pl.ds(pl.multiple_of(c * tk, tk), tk), :]

    bpc = tk // SEL_BLOCK
    em = lax.broadcasted_iota(jnp.int32, (LANE, tk), 0)
    ek = lax.broadcasted_iota(jnp.int32, (LANE, tk), 1) >> 6

    def get_expand(c):
        return jnp.where(em == ek + c * bpc, 1.0, 0.0).astype(BF16)

    o = _nsa_core(q, tpos, ckv_ref[0], get_chunk, get_expand, nchunks, tk, wkv, qi * tq - WINDOW, gate_ref[...],
                  m_ref, l_ref, acc_ref, tq=tq, n_cmp=SEQ // CMP_BLOCK, n_sel=SEQ // SEL_BLOCK, nsl=LANE,
                  keys_on_lanes=False, nsplit=nsplit)
    o_ref[...] = o.reshape(NSA_HEADS, tq, LANE)


def _nsa_prompt(nq, ckv, rowsb, wpad, gate, *, tq=128, tk=512, nsplit=2):
    nqt = SEQ // tq
    r = NSA_HEADS * tq
    return pl.pallas_call(
        functools.partial(_nsa_prompt_body, tq=tq, tk=tk, nsplit=nsplit),
        grid=(BATCH, nqt),
        in_specs=[
            pl.BlockSpec((NSA_HEADS, tq, LANE), lambda b, i: (0, b * nqt + i, 0)),
            pl.BlockSpec((1, 2 * LANE, LANE), lambda b, i: (b, 0, 0)),
            pl.BlockSpec((SEQ, LANE), lambda b, i: (b, 1)),
            pl.BlockSpec((1, WINDOW + SEQ, LANE), lambda b, i: (b, 0, 0)),
            pl.BlockSpec((tq, LANE), lambda b, i: (b * nqt + i, 0)),
        ],
        out_specs=pl.BlockSpec((NSA_HEADS, tq, LANE), lambda b, i: (0, b * nqt + i, 0)),
        out_shape=jax.ShapeDtypeStruct((NSA_HEADS, MP, LANE), F32),
        scratch_shapes=[pltpu.VMEM((r, LANE), F32), pltpu.VMEM((r, LANE), F32), pltpu.VMEM((r, LANE), F32)],
        compiler_params=_params(("arbitrary", "arbitrary")),
        name="nsa_prompt",
    )(nq, ckv, rowsb, wpad, gate)


_SAMPLE_LK = PAST_LEN + PAGE_SIZE


def _nsa_sample_body(pt_ref, q_ref, snew_ref, win_ref, wnew_ref, gate_ref,
                     perm_ref, pe_ref, w1_ref, w2_ref, peo_ref, cache_ref,
                     o_ref, pages_ref, sem_ref, g_ref, skv_ref, wkv_ref, m_ref, l_ref, acc_ref, *, layer):
    slot = _paged_prefetch(pt_ref, cache_ref, pages_ref, sem_ref, layer)
    half = 2 * NSA_HD
    for pq in range(N_PAGES // 4):
        xt = jnp.concatenate(
            [jnp.concatenate([pages_ref[slot, 4 * pq + 2 * a, 0:half, :],
                              pages_ref[slot, 4 * pq + 2 * a + 1, 0:half, :]], axis=1) for a in range(2)],
            axis=0).astype(BF16)
        gall = _dot_nt(perm_ref[...], xt)
        _scatter_pair(gall[:, 0:half], g_ref, 2 * pq)
        _scatter_pair(gall[:, half:2 * half], g_ref, 2 * pq + 1)
    ckv = _compress_finish(g_ref, pe_ref, w1_ref, w2_ref, peo_ref, nb=PAST_LEN // CMP_BLOCK)
    for i in range(N_PAGES):
        skv_ref[:, i * PAGE_SIZE:(i + 1) * PAGE_SIZE] = pages_ref[slot, i, half:2 * half, :].astype(BF16)
    skv_ref[:, PAST_LEN:_SAMPLE_LK] = snew_ref[0]
    wkv_ref[:, 0:WINDOW] = win_ref[...].astype(BF16)
    wkv_ref[:, WINDOW:WINDOW + PAGE_SIZE] = wnew_ref[0]
    tpos = PAST_LEN + lax.broadcasted_iota(jnp.int32, (TS, 1), 0)
    n_sel = -(-(PAST_LEN + DEC_SEQ) // SEL_BLOCK)

    o = _nsa_core(q_ref[0], tpos, ckv, lambda c: skv_ref[...], None, 1, _SAMPLE_LK,
                  wkv_ref[...], PAST_LEN - WINDOW, gate_ref[0],
                  m_ref, l_ref, acc_ref, tq=TS, n_cmp=PAST_LEN // CMP_BLOCK, n_sel=n_sel, nsl=2 * LANE,
                  keys_on_lanes=True, nsplit=1)
    o_ref[0] = o
    _paged_drain(cache_ref, pages_ref, sem_ref, layer, slot)


def _nsa_sample(page_flat, qs, snew_t, win_state_t, wnew_t, gates, perm, pe, w1, w2, peo,
                cache_nsa_t, layer):
    r = NSA_HEADS * TS
    nb = PAST_LEN // CMP_BLOCK
    b3 = lambda b, pt: (b, 0, 0)
    c2 = lambda b, pt: (0, 0)
    grid_spec = pltpu.PrefetchScalarGridSpec(
        num_scalar_prefetch=1,
        grid=(DEC_BATCH,),
        in_specs=[
            pl.BlockSpec((1, r, LANE), b3),
            pl.BlockSpec((1, LANE, PAGE_SIZE), b3),
            pl.BlockSpec((None, None, LANE, WINDOW), lambda b, pt: (layer, b, 0, 0)),
            pl.BlockSpec((1, LANE, PAGE_SIZE), b3),
            pl.BlockSpec((1, TS, LANE), b3),
            pl.BlockSpec((_PAIR, _PAIR), c2),
            pl.BlockSpec((CMP_BLOCK // 2, 2 * LANE), c2),
            pl.BlockSpec((CMP_BLOCK // 2, 2 * LANE, 2 * CMP_HIDDEN), lambda b, pt: (0, 0, 0)),
            pl.BlockSpec((2 * CMP_HIDDEN, LANE), c2),
            pl.BlockSpec((_NCL, nb), c2),
            pl.BlockSpec(memory_space=pl.ANY),
        ],
        out_specs=pl.BlockSpec((1, r, LANE), b3),
        scratch_shapes=[
            pltpu.VMEM((2, N_PAGES, 4 * NSA_HD, PAGE_SIZE), F32),
            pltpu.SemaphoreType.DMA((2,)),
            pltpu.VMEM((CMP_BLOCK // 2, nb + 8, 2 * LANE), F32),
            pltpu.VMEM((LANE, _SAMPLE_LK), BF16),
            pltpu.VMEM((LANE, WINDOW + PAGE_SIZE), BF16),
            pltpu.VMEM((r, LANE), F32), pltpu.VMEM((r, LANE), F32), pltpu.VMEM((r, LANE), F32),
        ],
    )
    return pl.pallas_call(
        functools.partial(_nsa_sample_body, layer=layer),
        grid_spec=grid_spec,
        out_shape=jax.ShapeDtypeStruct((DEC_BATCH, r, LANE), F32),
        compiler_params=_params(("arbitrary",)),
        name="nsa_sample",
    )(page_flat, qs, snew_t, win_state_t, wnew_t, gates, perm, pe, w1, w2, peo, cache_nsa_t)


_CONV_PAD = 32
CONV_PAD_W = 32


def _conv_rows(xbuf, w, bias, g, beta, t_out):
    acc = jnp.zeros((t_out, CONV_CH), F32) + bias
    off = _CONV_PAD - (CONV_W - 1)
    for k in range(CONV_W):
        acc = acc + xbuf[off + k:off + k + t_out, :] * w[k:k + 1, :]
    mu = jnp.mean(acc, axis=-1, keepdims=True)
    var = jnp.mean(jnp.square(acc - mu), axis=-1, keepdims=True)
    y = (acc - mu) * lax.rsqrt(var + EPS) * g + beta
    return y * jax.nn.sigmoid(y)


def _conv_prompt_body(main_ref, halo_ref, w_ref, b_ref, g_ref, beta_ref, o_ref, *, tt):
    xbuf = jnp.concatenate([main_ref[0], halo_ref[0]], axis=0)
    o_ref[0] = _conv_rows(xbuf, w_ref[...], b_ref[...], g_ref[...], beta_ref[...], tt)


def _conv_prompt(buf, w, b, g, beta, *, tt=256):
    nt = SEQ // tt
    vec = lambda bb, j: (0, 0)
    return pl.pallas_call(
        functools.partial(_conv_prompt_body, tt=tt),
        grid=(BATCH, nt),
        in_specs=[
            pl.BlockSpec((1, tt, CONV_CH), lambda bb, j: (bb, j, 0)),
            pl.BlockSpec((1, _CONV_PAD, CONV_CH), lambda bb, j: (bb, (j + 1) * (tt // _CONV_PAD), 0)),
            pl.BlockSpec((CONV_PAD_W, CONV_CH), vec),
            pl.BlockSpec((1, CONV_CH), vec),
            pl.BlockSpec((1, CONV_CH), vec),
            pl.BlockSpec((1, CONV_CH), vec),
        ],
        out_specs=pl.BlockSpec((1, tt, CONV_CH), lambda bb, j: (bb, j, 0)),
        out_shape=jax.ShapeDtypeStruct((BATCH, SEQ, CONV_CH), F32),
        compiler_params=_params(("arbitrary", "arbitrary")),
        name="conv_prompt",
    )(buf, buf, w, b, g, beta)


def _conv_sample_body(buf_ref, w_ref, b_ref, g_ref, beta_ref, o_ref, *, nb):
    for i in range(nb):
        o_ref[i] = _conv_rows(buf_ref[i], w_ref[...], b_ref[...], g_ref[...], beta_ref[...], TS)


def _conv_sample(buf, w, b, g, beta, *, nb=16):
    rows = buf.shape[1]
    vec = lambda i: (0, 0)
    return pl.pallas_call(
        functools.partial(_conv_sample_body, nb=nb),
        grid=(DEC_BATCH // nb,),
        in_specs=[
            pl.BlockSpec((nb, rows, CONV_CH), lambda i: (i, 0, 0)),
            pl.BlockSpec((CONV_PAD_W, CONV_CH), vec),
            pl.BlockSpec((1, CONV_CH), vec),
            pl.BlockSpec((1, CONV_CH), vec),
            pl.BlockSpec((1, CONV_CH), vec),
        ],
        out_specs=pl.BlockSpec((nb, TS, CONV_CH), lambda i: (i, 0, 0)),
        out_shape=jax.ShapeDtypeStruct((DEC_BATCH, TS, CONV_CH), F32),
        compiler_params=_params(("arbitrary",)),
        name="conv_sample",
    )(buf, w, b, g, beta)


def _merge_body(x_ref, olp_ref, ols_ref, onp_ref, ons_ref, ocp_ref, ocs_ref, wuv_ref, gn_ref, wo_ref, o_ref, *, np_tiles):
    is_sample = pl.program_id(0) == np_tiles
    ol = jnp.where(is_sample, ols_ref[...], olp_ref[...])
    on = jnp.where(is_sample, ons_ref[...], onp_ref[...])
    oc = jnp.where(is_sample, ocs_ref[...], ocp_ref[...])
    om = jnp.concatenate(
        [_dot(ol[:, h * MLA_KV_RANK:(h + 1) * MLA_KV_RANK], wuv_ref[h]) for h in range(MLA_HEADS)], axis=1)
    a1 = MLA_WIDTH
    a2 = MLA_WIDTH + NSA_WIDTH
    parts = jnp.concatenate([
        _rms(om, gn_ref[:, 0:a1]).astype(BF16),
        _rms(on, gn_ref[:, a1:a2]).astype(BF16),
        _rms(oc, gn_ref[:, a2:D_MODEL]).astype(BF16),
    ], axis=1)
    o_ref[...] = x_ref[...] + _dot(parts, wo_ref[...])


def _merge(x, ol_p, ol_s, on_p, on_s, oc_p, oc_s, wuv, gn, wo_all, layer, *, tm=MS):
    m = x.shape[0]
    np_tiles = MP // tm
    row = lambda i: (i, 0)
    prow = lambda i: (jnp.minimum(i, np_tiles - 1), 0)
    srow = lambda i: (0, 0)
    return pl.pallas_call(
        functools.partial(_merge_body, np_tiles=np_tiles),
        grid=(m // tm,),
        in_specs=[
            pl.BlockSpec((tm, D_MODEL), row),
            pl.BlockSpec((tm, MLA_HEADS * MLA_KV_RANK), prow),
            pl.BlockSpec((tm, MLA_HEADS * MLA_KV_RANK), srow),
            pl.BlockSpec((tm, NSA_WIDTH), prow),
            pl.BlockSpec((tm, NSA_WIDTH), srow),
            pl.BlockSpec((tm, CONV_CH), prow),
            pl.BlockSpec((tm, CONV_CH), srow),
            pl.BlockSpec((MLA_HEADS, MLA_KV_RANK, MLA_V), lambda i: (0, 0, 0)),
            pl.BlockSpec((1, D_MODEL), lambda i: (0, 0)),
            pl.BlockSpec((None, D_MODEL, D_MODEL), lambda i: (layer, 0, 0)),
        ],
        out_specs=pl.BlockSpec((tm, D_MODEL), row),
        out_shape=jax.ShapeDtypeStruct((m, D_MODEL), F32),
        compiler_params=_params(("arbitrary",)),
        name="merge",
    )(x, ol_p, ol_s, on_p, on_s, oc_p, oc_s, wuv, gn, wo_all)


def _mlp_body(x_ref, g_ref, w1_ref, w2_ref, o_ref, xn_ref, acc_ref):
    j = pl.program_id(1)

    @pl.when(j == 0)
    def _():
        xn_ref[...] = _rms(x_ref[...], g_ref[...]).astype(BF16)
        acc_ref[...] = jnp.zeros_like(acc_ref)

    h = jnp.maximum(_dot(xn_ref[...], w1_ref[...]), 0.0)
    acc_ref[...] += _dot((h * h).astype(BF16), w2_ref[...])

    @pl.when(j == pl.num_programs(1) - 1)
    def _():
        o_ref[...] = x_ref[...] + acc_ref[...]


def _mlp(x, g, w1_all, w2_all, layer, *, tm=512, tf=1024):
    m = x.shape[0]
    return pl.pallas_call(
        _mlp_body,
        grid=(m // tm, D_FF // tf),
        in_specs=[
            pl.BlockSpec((tm, D_MODEL), lambda i, j: (i, 0)),
            pl.BlockSpec((1, D_MODEL), lambda i, j: (0, 0)),
            pl.BlockSpec((None, D_MODEL, tf), lambda i, j: (layer, 0, j)),
            pl.BlockSpec((None, tf, D_MODEL), lambda i, j: (layer, j, 0)),
        ],
        out_specs=pl.BlockSpec((tm, D_MODEL), lambda i, j: (i, 0)),
        out_shape=jax.ShapeDtypeStruct((m, D_MODEL), F32),
        scratch_shapes=[pltpu.VMEM((tm, D_MODEL), BF16), pltpu.VMEM((tm, D_MODEL), F32)],
        compiler_params=_params(("arbitrary", "arbitrary")),
        name="mlp",
    )(x, g, w1_all, w2_all)


def _final_norm_body(x_ref, g_ref, o_ref):
    o_ref[...] = _rms(x_ref[...], g_ref[...])


def _final_norm(x, g, *, tm=512):
    m = x.shape[0]
    return pl.pallas_call(
        _final_norm_body,
        grid=(m // tm,),
        in_specs=[pl.BlockSpec((tm, D_MODEL), lambda i: (i, 0)), pl.BlockSpec((1, D_MODEL), lambda i: (0, 0))],
        out_specs=pl.BlockSpec((tm, D_MODEL), lambda i: (i, 0)),
        out_shape=jax.ShapeDtypeStruct((m, D_MODEL), F32),
        compiler_params=_params(("arbitrary",)),
        name="final_norm",
    )(x, g)


def _pad_rows(w, n):
    return jnp.pad(w, ((0, n - w.shape[0]), (0, 0)))


def _pack_w_in(w_t):
    segs = [
        w_t[OFF_CQ:OFF_CKV],
        w_t[OFF_CKV:OFF_KR],
        _pad_rows(w_t[OFF_KR:OFF_NQ], LANE),
        w_t[OFF_NQ:OFF_NKV],
        w_t[OFF_NKV:OFF_GATE],
        _pad_rows(w_t[OFF_GATE:OFF_CONV], LANE),
        w_t[OFF_CONV:IN_COLS],
    ]
    return jnp.concatenate(segs, axis=0).astype(BF16)


def _pack_w_q_up(w):
    w = w.reshape(MLA_Q_RANK, MLA_HEADS, MLA_NOPE + MLA_ROPE)
    nope = w[:, :, :MLA_NOPE].reshape(MLA_Q_RANK, MLA_HEADS * MLA_NOPE)
    rope = w[:, :, MLA_NOPE:].reshape(MLA_Q_RANK, MLA_HEADS * MLA_ROPE)
    return jnp.concatenate([nope, rope], axis=1).astype(BF16)


def _pack_cmp(pe, w1, w2):
    pe_t = jnp.concatenate([pe[0], pe[1]], axis=1)
    w1k = w1[0].reshape(CMP_BLOCK, NSA_HD, CMP_HIDDEN)
    w1v = w1[1].reshape(CMP_BLOCK, NSA_HD, CMP_HIDDEN)
    zero = jnp.zeros_like(w1k)
    w1b = jnp.concatenate([jnp.concatenate([w1k, zero], axis=2),
                           jnp.concatenate([zero, w1v], axis=2)], axis=1)
    z2 = jnp.zeros_like(w2[0])
    w2b = jnp.concatenate([jnp.concatenate([w2[0], z2], axis=1),
                           jnp.concatenate([z2, w2[1]], axis=1)], axis=0)
    pe2 = pe_t.reshape(CMP_BLOCK // 2, 2 * LANE)
    w1p = w1b.reshape(CMP_BLOCK // 2, 2 * LANE, 2 * CMP_HIDDEN)
    return pe2, w1p.astype(BF16), w2b.astype(BF16)


def _rope_tables():
    half = NSA_HD // 2
    freq = ROPE_THETA ** (-jnp.arange(half, dtype=F32) / half)
    pos = jnp.concatenate([jnp.arange(SEQ, dtype=jnp.int32),
                           PAST_LEN + (jnp.arange(256, dtype=jnp.int32) % DEC_SEQ)])
    ang = pos.astype(F32)[:, None] * freq[None, :]
    cos = jnp.tile(jnp.cos(ang), (1, LANE // half))
    sin = jnp.tile(jnp.sin(ang), (1, LANE // half))
    return cos, sin


def kernel(x_prompt, x_sample, cache_mla, cache_nsa, state_nsa_win, state_conv, page_table, norm_mix, w_in, mla_q_norm, mla_kv_norm, mla_w_q_up, mla_w_uk, mla_w_uv, nsa_cmp_pe, nsa_cmp_w1, nsa_cmp_w2, conv_w, conv_b, conv_ln_g, conv_ln_b, group_norm, w_out, norm_mlp, mlp_w1, mlp_w2, norm_final):
    x = jnp.concatenate([x_prompt.reshape(MP, D_MODEL), x_sample.reshape(MS, D_MODEL)], axis=0)
    cos_t, sin_t = _rope_tables()
    page_flat = page_table.reshape(-1).astype(jnp.int32)
    n_pool = cache_nsa.shape[1]
    cache_mla_t = jnp.swapaxes(cache_mla, 2, 3)
    cache_nsa_t = jnp.transpose(cache_nsa, (0, 1, 3, 4, 5, 2)).reshape(DEPTH, n_pool, 4 * NSA_HD, PAGE_SIZE)
    win_t = jnp.transpose(state_nsa_win, (0, 1, 3, 4, 5, 2)).reshape(DEPTH, DEC_BATCH, 2 * NSA_HD, WINDOW)
    perm, peo_p = _perm_matrices(SEQ // CMP_BLOCK)
    _, peo_s = _perm_matrices(PAST_LEN // CMP_BLOCK)
    w_out_b, mlp_w1_b, mlp_w2_b = w_out.astype(BF16), mlp_w1.astype(BF16), mlp_w2.astype(BF16)
    w_in_t = jnp.transpose(w_in, (2, 0, 1))

    mla_p, mla_s, nsa_p, nsa_s, win_p, win_s, conv_p, conv_s = [], [], [], [], [], [], [], []
    for l in range(DEPTH):
        wp = _pack_w_in(w_in_t[:, l, :])
        wq = _pack_w_q_up(mla_w_q_up[l])
        wuk = jnp.transpose(mla_w_uk[l], (1, 2, 0)).astype(BF16)
        wuv = jnp.transpose(mla_w_uv[l], (1, 0, 2)).astype(BF16)
        pe_t, w1b, w2b = _pack_cmp(nsa_cmp_pe[l], nsa_cmp_w1[l], nsa_cmp_w2[l])
        cw = jnp.pad(conv_w[l], ((0, CONV_PAD_W - CONV_W), (0, 0)))

        z = _inproj(x, norm_mix[l][None], wp)
        q384, kvb, mla_row, nq, rows, rowsb, gate, u = _post(
            z, cos_t, sin_t, mla_q_norm[l][None], mla_kv_norm[l][None], wq, wuk)

        mla_p.append(mla_row[:MP, :MLA_ROW].reshape(BATCH, SEQ, MLA_ROW))
        mla_s.append(mla_row[MP:, :MLA_ROW].reshape(DEC_BATCH, DEC_SEQ, MLA_ROW))
        nsa_p.append(rows[:MP, :4 * NSA_HD].reshape(BATCH, SEQ, 4, 1, NSA_HD))
        nsa_s.append(rows[MP:, :4 * NSA_HD].reshape(DEC_BATCH, DEC_SEQ, 4, 1, NSA_HD))
        win_rows_p = rows[:MP, 4 * NSA_HD:].reshape(BATCH, SEQ, 2, 1, NSA_HD)
        win_p.append(win_rows_p[:, SEQ - WINDOW:])
        win_rows_s = rows[MP:, 4 * NSA_HD:].reshape(DEC_BATCH, DEC_SEQ, 2, 1, NSA_HD)
        win_s.append(jnp.concatenate([state_nsa_win[l][:, DEC_SEQ:], win_rows_s], axis=1))
        up = u[:MP].reshape(BATCH, SEQ, CONV_CH)
        us = u[MP:].reshape(DEC_BATCH, DEC_SEQ, CONV_CH)
        bufs = jnp.concatenate([state_conv[l], us], axis=1)
        conv_p.append(up[:, SEQ - (CONV_W - 1):])
        conv_s.append(bufs[:, DEC_SEQ:])

        ol_p = _mla_prompt(q384, kvb)
        qs = q384[:, MP:].reshape(MLA_HEADS, DEC_BATCH, DEC_SEQ, KV_PAD)
        qs = jnp.transpose(qs, (1, 0, 2, 3)).reshape(DEC_BATCH, MLA_HEADS * DEC_SEQ, KV_PAD)
        knew = jnp.pad(kvb[MP:].reshape(DEC_BATCH, DEC_SEQ, KV_PAD), ((0, 0), (0, PAGE_SIZE - DEC_SEQ), (0, 0)))
        ol_s = _mla_sample(page_flat, qs, jnp.swapaxes(knew, 1, 2), cache_mla_t, l)
        ol_s = ol_s.reshape(DEC_BATCH, MLA_HEADS, DEC_SEQ, MLA_KV_RANK)
        ol_s = jnp.transpose(ol_s, (0, 2, 1, 3)).reshape(MS, MLA_HEADS * MLA_KV_RANK)

        ckv_p = _compress_prompt(rows, perm, pe_t, w1b, w2b, peo_p)
        wpad = jnp.pad(rowsb[:MP, 2 * LANE:].reshape(BATCH, SEQ, LANE), ((0, 0), (WINDOW, 0), (0, 0)))
        on_p = _nsa_prompt(nq, ckv_p, rowsb, wpad, gate)
        on_p = jnp.transpose(on_p[:, :, NSA_HD:], (1, 0, 2)).reshape(MP, NSA_WIDTH)

        pad_tok = ((0, 0), (0, 0), (0, TS - DEC_SEQ), (0, 0))
        nqs = jnp.pad(nq[:, MP:].reshape(NSA_HEADS, DEC_BATCH, DEC_SEQ, LANE), pad_tok)
        nqs = jnp.transpose(nqs, (1, 0, 2, 3)).reshape(DEC_BATCH, NSA_HEADS * TS, LANE)
        pad_rows = ((0, 0), (0, PAGE_SIZE - DEC_SEQ), (0, 0))
        snew = jnp.pad(rowsb[MP:, LANE:2 * LANE].reshape(DEC_BATCH, DEC_SEQ, LANE), pad_rows)
        wnew = jnp.pad(rowsb[MP:, 2 * LANE:].reshape(DEC_BATCH, DEC_SEQ, LANE), pad_rows)
        gs = jnp.pad(gate[MP:].reshape(DEC_BATCH, DEC_SEQ, LANE), ((0, 0), (0, TS - DEC_SEQ), (0, 0)))
        on_s = _nsa_sample(page_flat, nqs, jnp.swapaxes(snew, 1, 2), win_t, jnp.swapaxes(wnew, 1, 2),
                           gs, perm, pe_t, w1b, w2b, peo_s, cache_nsa_t, l)
        on_s = on_s[:, :, NSA_HD:].reshape(DEC_BATCH, NSA_HEADS, TS, NSA_HD)[:, :, :DEC_SEQ]
        on_s = jnp.transpose(on_s, (0, 2, 1, 3)).reshape(MS, NSA_WIDTH)

        bufp = jnp.pad(up, ((0, 0), (_CONV_PAD, 0), (0, 0)))
        oc_p = _conv_prompt(bufp, cw, conv_b[l][None], conv_ln_g[l][None], conv_ln_b[l][None])
        bufs_p = jnp.pad(bufs, ((0, 0), (_CONV_PAD - (CONV_W - 1), TS - DEC_SEQ), (0, 0)))
        oc_s = _conv_sample(bufs_p, cw, conv_b[l][None], conv_ln_g[l][None], conv_ln_b[l][None])
        oc_p = oc_p.reshape(MP, CONV_CH)
        oc_s = oc_s[:, :DEC_SEQ].reshape(MS, CONV_CH)

        x = _merge(x, ol_p, ol_s, on_p, on_s, oc_p, oc_s, wuv, group_norm[l][None], w_out_b, l)
        x = _mlp(x, norm_mlp[l][None], mlp_w1_b, mlp_w2_b, l)

    y = _final_norm(x, norm_final[None])
    return (y[:MP].reshape(BATCH, SEQ, D_MODEL), y[MP:].reshape(DEC_BATCH, DEC_SEQ, D_MODEL),
            jnp.stack(mla_p), jnp.stack(mla_s), jnp.stack(nsa_p), jnp.stack(nsa_s),
            jnp.stack(win_p), jnp.stack(win_s), jnp.stack(conv_p), jnp.stack(conv_s))
```
